```python
import jax, jax.numpy as jnp
from jax import lax
import numpy as np

D_MODEL = 1024
BATCH = 4
SEQ = 4096
DEPTH = 2
DEC_BATCH = 32
DEC_SEQ = 4
PAST_LEN = 16384
PAGE_SIZE = 128

HEAD_DIM = 64
N_BRANCH = 4
BRANCH_W = D_MODEL // N_BRANCH
W_LRU = BRANCH_W
LRU_BLOCKS = BRANCH_W // HEAD_DIM
LRU_BS = W_LRU // LRU_BLOCKS
LRU_C = 8.0
CONV_W = 4
H_SB = BRANCH_W // HEAD_DIM
H_FOX = BRANCH_W // HEAD_DIM
H_RW = BRANCH_W // HEAD_DIM
D_SB = H_SB * HEAD_DIM
D_FOX = H_FOX * HEAD_DIM
D_RW = H_RW * HEAD_DIM
RW_RANK_W = D_MODEL // 32
RW_RANK_A = D_MODEL // 32
RW_RANK_G = D_MODEL // 16
D_RW_IN = 3 * D_RW + RW_RANK_W + RW_RANK_A + RW_RANK_G
IN_SPLITS = (W_LRU, W_LRU, D_SB, D_SB, D_SB, D_FOX, D_FOX, D_FOX, H_FOX, D_RW_IN, N_BRANCH * D_MODEL)
D_IN = sum(IN_SPLITS)
RW_SPLITS = (D_RW, D_RW, D_RW, RW_RANK_W, RW_RANK_A, RW_RANK_G)
N_GROUPS = 4
N_EXP_PER_GROUP = 4
N_EXPERTS = N_GROUPS * N_EXP_PER_GROUP
TOP_K = 2
D_EXP = D_MODEL // 4
Q_BLOCK = 128
RMS_EPS = 1e-6
RW_GN_EPS = 64e-5
FOX_FBIAS_INIT = 3.0
POOL_NUM = 5
POOL_DEN = 4

kernel_name = 'hybrid_lru_sb_fox_rwkv7_hmoe_step'


def rmsnorm(x, g):
    xf = x.astype(jnp.float32)
    return xf * lax.rsqrt(jnp.mean(xf * xf, axis=-1, keepdims=True) + RMS_EPS) * g


def split_cols(a, sizes):
    return jnp.split(a, np.cumsum(sizes)[:-1].tolist(), axis=-1)


def gather_pages(pool, page_table):
    rows = pool[page_table]
    return rows.reshape(rows.shape[0], rows.shape[1] * rows.shape[2], *rows.shape[3:]).astype(jnp.float32)


def sweep_query_blocks(fn, q, q_pos, q_aux):
    B, Tq = q.shape[0], q.shape[1]
    if Tq <= Q_BLOCK:
        return fn(q, q_pos, q_aux)
    nb = Tq // Q_BLOCK
    def to_blocks(a):
        return jnp.moveaxis(a.reshape(B, nb, Q_BLOCK, *a.shape[2:]), 1, 0)
    aux_b = None if q_aux is None else to_blocks(q_aux)
    out = lax.map(lambda args: fn(*args), (to_blocks(q), q_pos.reshape(nb, Q_BLOCK), aux_b))
    return jnp.moveaxis(out, 0, 1).reshape(B, Tq, *out.shape[3:])


def stick_breaking_block(q, k, v, q_pos, k_pos):
    z = jnp.einsum('bqhd,bkhd->bhqk', q, k) * (HEAD_DIM ** -0.5)
    mask = k_pos[None, :] < q_pos[:, None]
    log_stay = jnp.where(mask, jax.nn.log_sigmoid(-z), 0.0)
    later = lax.cumsum(log_stay, axis=3, reverse=True) - log_stay
    w = jnp.where(mask, jnp.exp(jax.nn.log_sigmoid(z) + later), 0.0)
    return jnp.einsum('bhqk,bkhd->bqhd', w, v)


def forgetting_block(q, k, v, fq, fk, q_pos, k_pos):
    s = jnp.einsum('bqhd,bkhd->bhqk', q, k) * (HEAD_DIM ** -0.5)
    s = s + jnp.moveaxis(fq, 1, 2)[..., :, None] - jnp.moveaxis(fk, 1, 2)[..., None, :]
    mask = k_pos[None, :] <= q_pos[:, None]
    p = jax.nn.softmax(jnp.where(mask, s, -jnp.inf), axis=-1)
    return jnp.einsum('bhqk,bkhd->bqhd', p, v)


def rglru_branch(x_in, y_in, conv_buf, h0, pl):
    B, T, W = x_in.shape
    xp = jnp.concatenate([conv_buf.astype(jnp.float32), x_in], axis=1)
    xc = pl['lru_conv_b'] + pl['lru_conv_w'][CONV_W - 1] * xp[:, CONV_W - 1:]
    for i in range(CONV_W - 1):
        xc = xc + pl['lru_conv_w'][i] * xp[:, i:i + T]
    xb = xc.reshape(B, T, LRU_BLOCKS, LRU_BS)
    r = jax.nn.sigmoid(jnp.einsum('btnc,ncd->btnd', xb, pl['lru_wa']).reshape(B, T, W) + pl['lru_ba'])
    i_g = jax.nn.sigmoid(jnp.einsum('btnc,ncd->btnd', xb, pl['lru_wi']).reshape(B, T, W) + pl['lru_bi'])
    log_a = -LRU_C * r * jax.nn.softplus(-pl['lru_lambda'])
    a = jnp.exp(log_a)
    u = jnp.sqrt(-jnp.expm1(2.0 * log_a)) * (i_g * xc)
    u = u.at[:, 0].add(a[:, 0] * h0.astype(jnp.float32))
    _, h = lax.associative_scan(lambda l, m: (l[0] * m[0], m[0] * l[1] + m[1]), (a, u), axis=1)
    return h * jax.nn.gelu(y_in), xp[:, T:], h[:, -1]


def rwkv7_branch(pd, shift0, S0, pl):
    B, T, _ = pd.shape
    prev = jnp.concatenate([shift0[:, None].astype(jnp.float32), pd[:, :-1]], axis=1)
    ps = pd + (prev - pd) * pl['rw_mu']
    r, k, v, wd, ad, gd = split_cols(ps, RW_SPLITS)
    w_log = -jax.nn.softplus(-(pl['rw_w0'] + jnp.tanh(wd) @ pl['rw_w2'])) - 0.5
    decay = jnp.exp(-jnp.exp(w_log))
    a = jax.nn.sigmoid(pl['rw_a0'] + ad @ pl['rw_a2'])
    g = jax.nn.sigmoid(gd) @ pl['rw_g2']
    hd = lambda t: t.reshape(B, T, H_RW, HEAD_DIM)
    r, k, v, decay, a = hd(r), hd(k), hd(v), hd(decay), hd(a)
    kk = k * pl['rw_kk'].reshape(H_RW, HEAD_DIM)
    kk = kk / jnp.maximum(jnp.sqrt(jnp.sum(kk * kk, axis=-1, keepdims=True)), 1e-12)
    k = k * (1.0 + (a - 1.0) * pl['rw_ka'].reshape(H_RW, HEAD_DIM))
    def step(S, inp):
        r_t, w_t, k_t, v_t, kk_t, a_t = inp
        sa = jnp.einsum('bhvk,bhk->bhv', S, -kk_t)
        S = S * w_t[:, :, None, :] + sa[..., None] * (kk_t * a_t)[:, :, None, :] + v_t[..., None] * k_t[:, :, None, :]
        return S, jnp.einsum('bhvk,bhk->bhv', S, r_t)
    xs = tuple(jnp.moveaxis(t, 1, 0) for t in (r, decay, k, v, kk, a))
    S_T, y = lax.scan(step, S0.astype(jnp.float32), xs)
    y = jnp.moveaxis(y, 0, 1)
    mu = jnp.mean(y, axis=-1, keepdims=True)
    var = jnp.mean(jnp.square(y - mu), axis=-1, keepdims=True)
    y = ((y - mu) * lax.rsqrt(var + RW_GN_EPS)).reshape(B, T, D_RW) * pl['rw_ln_g'] + pl['rw_ln_b']
    bonus = jnp.sum(r * k * pl['rw_rk'], axis=-1, keepdims=True) * v
    y = (y + bonus.reshape(B, T, D_RW)) * g
    return y, pd[:, -1], S_T


def hier_moe(x, pl):
    B, T, D = x.shape
    xt = x.reshape(B * T, D)
    g_logits = xt @ pl['moe_wg'] + pl['moe_bg']
    g_prob = jax.nn.softmax(g_logits, axis=-1)
    g_top, g_idx = lax.top_k(g_logits, 1)
    g_w = jnp.take_along_axis(g_prob, g_idx, axis=-1)
    e_logits = (xt @ pl['moe_we'] + pl['moe_be']).reshape(-1, N_GROUPS, N_EXP_PER_GROUP)
    e_in = jnp.take_along_axis(e_logits, g_idx[:, :, None], axis=1)[:, 0]
    top_v, top_i = lax.top_k(e_in, TOP_K)
    top_w = jax.nn.softmax(top_v, axis=-1) * g_w
    expert_id = g_idx * N_EXP_PER_GROUP + top_i
    gate = jnp.sum(jax.nn.one_hot(expert_id, N_EXPERTS, dtype=jnp.float32) * top_w[..., None], axis=1)
    h = jax.nn.silu(jnp.einsum('nd,edf->nef', xt, pl['moe_w1'])) * jnp.einsum('nd,edf->nef', xt, pl['moe_w3'])
    y = jnp.einsum('nef,efd->nd', h * gate[..., None], pl['moe_w2'])
    return y.reshape(B, T, D)


def trunk_layer(x, pl, lru_buf, lru_h, rw_shift, rw_wkv, sb_k_past, sb_v_past, fox_k_past, fox_v_past, fox_logf_past):
    B, T, _ = x.shape
    P = sb_k_past.shape[1]
    xn = rmsnorm(x, pl['norm1_g'])
    proj = xn @ pl['w_in']
    (lru_x, lru_y, sb_q, sb_k, sb_v, fx_q, fx_k, fx_v, fx_f, rw_in, gate_logits) = split_cols(proj, IN_SPLITS)
    hd = lambda t: t.reshape(B, T, -1, HEAD_DIM)
    k_pos = jnp.arange(P + T)
    q_pos = P + jnp.arange(T)
    y_a, lru_buf_new, lru_h_new = rglru_branch(lru_x, lru_y, lru_buf, lru_h, pl)
    sb_k, sb_v = hd(sb_k), hd(sb_v)
    sb_k_all = jnp.concatenate([sb_k_past.astype(jnp.float32), sb_k], axis=1)
    sb_v_all = jnp.concatenate([sb_v_past.astype(jnp.float32), sb_v], axis=1)
    y_b = sweep_query_blocks(lambda q, qp, _: stick_breaking_block(q, sb_k_all, sb_v_all, qp, k_pos), hd(sb_q), q_pos, None)
    fq = rmsnorm(hd(fx_q), pl['fox_qnorm_g'])
    fk = rmsnorm(hd(fx_k), pl['fox_knorm_g'])
    fv = hd(fx_v)
    logf = jax.nn.log_sigmoid(fx_f + pl['fox_fbias'])
    F_all = jnp.cumsum(jnp.concatenate([fox_logf_past.astype(jnp.float32), logf], axis=1), axis=1)
    fk_all = jnp.concatenate([fox_k_past.astype(jnp.float32), fk], axis=1)
    fv_all = jnp.concatenate([fox_v_past.astype(jnp.float32), fv], axis=1)
    y_c = sweep_query_blocks(lambda q, qp, fqb: forgetting_block(q, fk_all, fv_all, fqb, F_all, qp, k_pos), fq, q_pos, F_all[:, P:])
    y_d, rw_shift_new, rw_wkv_new = rwkv7_branch(rw_in, rw_shift, rw_wkv, pl)
    branches = jnp.stack([y_a, y_b.reshape(B, T, D_SB), y_c.reshape(B, T, D_FOX), y_d], axis=2)
    up = jnp.einsum('btnc,ncd->btnd', branches, pl['w_branch'])
    gates = jax.nn.sigmoid(gate_logits.reshape(B, T, N_BRANCH, D_MODEL))
    x = x + jnp.sum(gates * up, axis=2) @ pl['w_out']
    x = x + hier_moe(rmsnorm(x, pl['norm2_g']), pl)
    new_state = (sb_k, sb_v, fk, fv, logf, lru_buf_new, lru_h_new, rw_shift_new, rw_wkv_new)
    return x, new_state


def setup_inputs(seed: int = 0) -> dict:
    key = jax.random.key(seed)
    ks = iter(jax.random.split(key, 80))
    f32 = jnp.float32
    def nrm(shape, scale=1.0):
        return scale * jax.random.normal(next(ks), shape, f32)
    n_pages = PAST_LEN // PAGE_SIZE
    n_phys = (POOL_NUM * DEC_BATCH * n_pages) // POOL_DEN
    perm = jax.random.permutation(next(ks), n_phys)[:DEC_BATCH * n_pages]
    page_table = perm.reshape(DEC_BATCH, n_pages).astype(jnp.int32)
    u = jax.random.uniform(next(ks), (DEPTH, W_LRU), f32, 0.9, 0.999)
    s = u ** (1.0 / LRU_C)
    lru_lambda = jnp.log(s) - jnp.log1p(-s)
    return {
        'x_prompt': nrm((BATCH, SEQ, D_MODEL)),
        'x_sample': nrm((DEC_BATCH, DEC_SEQ, D_MODEL)),
        'cache_sb_k': nrm((DEPTH, n_phys, PAGE_SIZE, H_SB, HEAD_DIM)),
        'cache_sb_v': nrm((DEPTH, n_phys, PAGE_SIZE, H_SB, HEAD_DIM)),
        'cache_fox_k': nrm((DEPTH, n_phys, PAGE_SIZE, H_FOX, HEAD_DIM)),
        'cache_fox_v': nrm((DEPTH, n_phys, PAGE_SIZE, H_FOX, HEAD_DIM)),
        'cache_fox_logf': jax.nn.log_sigmoid(nrm((DEPTH, n_phys, PAGE_SIZE, H_FOX)) + FOX_FBIAS_INIT),
        'state_lru_conv': nrm((DEPTH, DEC_BATCH, CONV_W - 1, W_LRU)),
        'state_lru_h': nrm((DEPTH, DEC_BATCH, W_LRU), 0.5),
        'state_rwkv_shift': nrm((DEPTH, DEC_BATCH, D_RW_IN)),
        'state_rwkv_wkv': nrm((DEPTH, DEC_BATCH, H_RW, HEAD_DIM, HEAD_DIM), 0.3),
        'page_table': page_table,
        'norm1_g': 1.0 + nrm((DEPTH, D_MODEL), 0.02),
        'w_in': nrm((DEPTH, D_MODEL, D_IN), D_MODEL ** -0.5),
        'lru_conv_w': nrm((DEPTH, CONV_W, W_LRU), CONV_W ** -0.5),
        'lru_conv_b': nrm((DEPTH, W_LRU), 0.01),
        'lru_wa': nrm((DEPTH, LRU_BLOCKS, LRU_BS, LRU_BS), LRU_BS ** -0.5),
        'lru_ba': nrm((DEPTH, W_LRU), 0.01),
        'lru_wi': nrm((DEPTH, LRU_BLOCKS, LRU_BS, LRU_BS), LRU_BS ** -0.5),
        'lru_bi': nrm((DEPTH, W_LRU), 0.01),
        'lru_lambda': lru_lambda,
        'fox_qnorm_g': 1.0 + nrm((DEPTH, HEAD_DIM), 0.02),
        'fox_knorm_g': 1.0 + nrm((DEPTH, HEAD_DIM), 0.02),
        'fox_fbias': FOX_FBIAS_INIT + nrm((DEPTH, H_FOX), 0.1),
        'rw_mu': jax.random.uniform(next(ks), (DEPTH, D_RW_IN), f32),
        'rw_w0': -1.0 + nrm((DEPTH, D_RW), 0.5),
        'rw_w2': nrm((DEPTH, RW_RANK_W, D_RW), 0.1),
        'rw_a0': nrm((DEPTH, D_RW), 0.1),
        'rw_a2': nrm((DEPTH, RW_RANK_A, D_RW), RW_RANK_A ** -0.5),
        'rw_g2': nrm((DEPTH, RW_RANK_G, D_RW), RW_RANK_G ** -0.5),
        'rw_kk': 0.85 + nrm((DEPTH, D_RW), 0.05),
        'rw_ka': 1.0 + nrm((DEPTH, D_RW), 0.05),
        'rw_rk': nrm((DEPTH, H_RW, HEAD_DIM), 0.1),
        'rw_ln_g': 1.0 + nrm((DEPTH, D_RW), 0.02),
        'rw_ln_b': nrm((DEPTH, D_RW), 0.01),
        'w_branch': nrm((DEPTH, N_BRANCH, BRANCH_W, D_MODEL), BRANCH_W ** -0.5),
        'w_out': nrm((DEPTH, D_MODEL, D_MODEL), D_MODEL ** -0.5),
        'norm2_g': 1.0 + nrm((DEPTH, D_MODEL), 0.02),
        'moe_wg': nrm((DEPTH, D_MODEL, N_GROUPS), D_MODEL ** -0.5),
        'moe_bg': nrm((DEPTH, N_GROUPS), 0.01),
        'moe_we': nrm((DEPTH, D_MODEL, N_EXPERTS), D_MODEL ** -0.5),
        'moe_be': nrm((DEPTH, N_EXPERTS), 0.01),
        'moe_w1': nrm((DEPTH, N_EXPERTS, D_MODEL, D_EXP), D_MODEL ** -0.5),
        'moe_w3': nrm((DEPTH, N_EXPERTS, D_MODEL, D_EXP), D_MODEL ** -0.5),
        'moe_w2': nrm((DEPTH, N_EXPERTS, D_EXP, D_MODEL), D_EXP ** -0.5),
    }


def reference(x_prompt, x_sample, cache_sb_k, cache_sb_v, cache_fox_k, cache_fox_v, cache_fox_logf, state_lru_conv, state_lru_h, state_rwkv_shift, state_rwkv_wkv, page_table, norm1_g, w_in, lru_conv_w, lru_conv_b, lru_wa, lru_ba, lru_wi, lru_bi, lru_lambda, fox_qnorm_g, fox_knorm_g, fox_fbias, rw_mu, rw_w0, rw_w2, rw_a0, rw_a2, rw_g2, rw_kk, rw_ka, rw_rk, rw_ln_g, rw_ln_b, w_branch, w_out, norm2_g, moe_wg, moe_bg, moe_we, moe_be, moe_w1, moe_w3, moe_w2):
    f32 = jnp.float32
    params = {
        'norm1_g': norm1_g, 'w_in': w_in, 'lru_conv_w': lru_conv_w, 'lru_conv_b': lru_conv_b,
        'lru_wa': lru_wa, 'lru_ba': lru_ba, 'lru_wi': lru_wi, 'lru_bi': lru_bi, 'lru_lambda': lru_lambda,
        'fox_qnorm_g': fox_qnorm_g, 'fox_knorm_g': fox_knorm_g, 'fox_fbias': fox_fbias,
        'rw_mu': rw_mu, 'rw_w0': rw_w0, 'rw_w2': rw_w2, 'rw_a0': rw_a0, 'rw_a2': rw_a2, 'rw_g2': rw_g2,
        'rw_kk': rw_kk, 'rw_ka': rw_ka, 'rw_rk': rw_rk, 'rw_ln_g': rw_ln_g, 'rw_ln_b': rw_ln_b,
        'w_branch': w_branch, 'w_out': w_out, 'norm2_g': norm2_g,
        'moe_wg': moe_wg, 'moe_bg': moe_bg, 'moe_we': moe_we, 'moe_be': moe_be,
        'moe_w1': moe_w1, 'moe_w3': moe_w3, 'moe_w2': moe_w2,
    }
    B = x_prompt.shape[0]
    prompt_init = (
        jnp.zeros((B, CONV_W - 1, W_LRU), f32), jnp.zeros((B, W_LRU), f32),
        jnp.zeros((B, D_RW_IN), f32), jnp.zeros((B, H_RW, HEAD_DIM, HEAD_DIM), f32),
        jnp.zeros((B, 0, H_SB, HEAD_DIM), f32), jnp.zeros((B, 0, H_SB, HEAD_DIM), f32),
        jnp.zeros((B, 0, H_FOX, HEAD_DIM), f32), jnp.zeros((B, 0, H_FOX, HEAD_DIM), f32),
        jnp.zeros((B, 0, H_FOX), f32),
    )
    y_p, y_s = x_prompt, x_sample
    prompt_states, sample_states = [], []
    for l in range(DEPTH):
        pl = {name: arr[l] for name, arr in params.items()}
        y_p, st_p = trunk_layer(y_p, pl, *prompt_init)
        y_s, st_s = trunk_layer(
            y_s, pl, state_lru_conv[l], state_lru_h[l], state_rwkv_shift[l], state_rwkv_wkv[l],
            gather_pages(cache_sb_k[l], page_table), gather_pages(cache_sb_v[l], page_table),
            gather_pages(cache_fox_k[l], page_table), gather_pages(cache_fox_v[l], page_table),
            gather_pages(cache_fox_logf[l], page_table))
        prompt_states.append(st_p)
        sample_states.append(st_s)
    (sb_k_p, sb_v_p, fox_k_p, fox_v_p, fox_logf_p, lru_conv_p, lru_h_p, rw_shift_p, rw_wkv_p) = [jnp.stack([st[i] for st in prompt_states], axis=0) for i in range(9)]
    (sb_k_s, sb_v_s, fox_k_s, fox_v_s, fox_logf_s, lru_conv_s, lru_h_s, rw_shift_s, rw_wkv_s) = [jnp.stack([st[i] for st in sample_states], axis=0) for i in range(9)]
    return (y_p, y_s, sb_k_p, sb_v_p, fox_k_p, fox_v_p, fox_logf_p, lru_conv_p, lru_h_p, rw_shift_p, rw_wkv_p, sb_k_s, sb_v_s, fox_k_s, fox_v_s, fox_logf_s, lru_conv_s, lru_h_s, rw_shift_s, rw_wkv_s)
```

```python
import functools
import math

import jax
import jax.numpy as jnp
from jax import lax
from jax.experimental import pallas as pl
from jax.experimental.pallas import tpu as pltpu

F32 = jnp.float32
BF16 = jnp.bfloat16

D_MODEL = 1024
HEAD_DIM = 64
N_HEADS = 4
BRANCH_W = N_HEADS * HEAD_DIM
CONV_W = 4
LRU_C = 8.0
PAGE = 128
RW_TAIL = 128
D_RW_IN = 3 * BRANCH_W + RW_TAIL
N_MAIN = 8 * BRANCH_W
N_GROUPS = 4
N_EXPERTS = 16
D_EXP = 256
RMS_EPS = 1e-6
RW_GN_EPS = 64e-5
ATT_SCALE = HEAD_DIM ** -0.5
LANE = 128
RW_CHUNK = 64
VMEM_LIMIT = 56 * 1024 * 1024


def _cparams(sem):
    return pltpu.CompilerParams(dimension_semantics=sem, vmem_limit_bytes=VMEM_LIMIT)


def _dot(a, b):
    return jnp.dot(a.astype(BF16), b.astype(BF16), preferred_element_type=F32)


def _dot_nt(a, b):
    return lax.dot_general(a.astype(BF16), b.astype(BF16), (((1,), (1,)), ((), ())), preferred_element_type=F32)


def _dot_tn(a, b):
    return lax.dot_general(a.astype(BF16), b.astype(BF16), (((0,), (0,)), ((), ())), preferred_element_type=F32)


def _dot2(a, b_bf16):
    hi = a.astype(BF16)
    lo = (a - hi.astype(F32)).astype(BF16)
    return (jnp.dot(hi, b_bf16, preferred_element_type=F32) + jnp.dot(lo, b_bf16, preferred_element_type=F32))


def _softplus(x):
    return jnp.maximum(x, 0.0) + jnp.log1p(jnp.exp(-jnp.abs(x)))


def _sigmoid(x):
    return 1.0 / (1.0 + jnp.exp(-x))


def _gelu_tanh(x):
    return 0.5 * x * (1.0 + jnp.tanh(math.sqrt(2.0 / math.pi) * (x + 0.044715 * (x * x * x))))


def _rmsnorm(x, g):
    return x * lax.rsqrt(jnp.mean(x * x, axis=-1, keepdims=True) + RMS_EPS) * g


def _row_iota(shape):
    return lax.broadcasted_iota(jnp.int32, shape, 0)


def _col_iota(shape):
    return lax.broadcasted_iota(jnp.int32, shape, 1)


def _scan_rows(a, u):
    n = a.shape[0]
    row = _row_iota(a.shape)
    s = 1
    while s < n:
        valid = row >= s
        a_sh = pltpu.roll(a, s, 0)
        u_sh = pltpu.roll(u, s, 0)
        u = jnp.where(valid, a * u_sh + u, u)
        a = jnp.where(valid, a * a_sh, a)
        s *= 2
    return a, u


def _cumsum_rows(x):
    n = x.shape[0]
    row = _row_iota(x.shape)
    s = 1
    while s < n:
        x = x + jnp.where(row >= s, pltpu.roll(x, s, 0), 0.0)
        s *= 2
    return x


def _cumsum_lanes(x):
    n = x.shape[1]
    col = _col_iota(x.shape)
    s = 1
    while s < n:
        x = x + jnp.where(col >= s, pltpu.roll(x, s, 1), 0.0)
        s *= 2
    return x


def _head_ones():
    i = jnp.arange(BRANCH_W) // HEAD_DIM
    return (i[:, None] == i[None, :]).astype(BF16)


def _proj_kernel(x_ref, g_ref, w_ref, qg_ref, kg_ref, fb_ref, hs_ref,
                 lru_ref, rw_ref, sbq_ref, sbk_ref, sbv_ref, sbkb_ref, sbvb_ref,
                 fq_ref, fk_ref, fv_ref, fkb_ref, fvb_ref, logf_ref):
    xn = _rmsnorm(x_ref[...], g_ref[...]).astype(BF16)

    def seg(a, b):
        return jnp.dot(xn, w_ref[:, a:b], preferred_element_type=F32)

    W = BRANCH_W
    lru_ref[...] = seg(0, 2 * W)
    sbq_ref[...] = (seg(2 * W, 3 * W) * ATT_SCALE).astype(BF16)
    k = seg(3 * W, 4 * W)
    sbk_ref[...] = k
    sbkb_ref[...] = k.astype(BF16)
    v = seg(4 * W, 5 * W)
    sbv_ref[...] = v
    sbvb_ref[...] = v.astype(BF16)
    hs = hs_ref[...]
    q = seg(5 * W, 6 * W)
    q = q * lax.rsqrt(_dot2(q * q, hs) * (1.0 / HEAD_DIM) + RMS_EPS) * qg_ref[...]
    fq_ref[...] = (q * ATT_SCALE).astype(BF16)
    k = seg(6 * W, 7 * W)
    k = k * lax.rsqrt(_dot2(k * k, hs) * (1.0 / HEAD_DIM) + RMS_EPS) * kg_ref[...]
    fk_ref[...] = k
    fkb_ref[...] = k.astype(BF16)
    v = seg(7 * W, 8 * W)
    fv_ref[...] = v
    fvb_ref[...] = v.astype(BF16)
    rw_ref[...] = seg(N_MAIN, N_MAIN + D_RW_IN)
    f = seg(N_MAIN + D_RW_IN, N_MAIN + D_RW_IN + LANE) + fb_ref[...]
    logf_ref[...] = -_softplus(-f)


def _proj(x2d, lw, tm):
    n = x2d.shape[0]
    W = BRANCH_W
    row = lambda w: pl.BlockSpec((tm, w), lambda i: (i, 0))
    full = lambda a: pl.BlockSpec(a.shape, lambda i: (0,) * a.ndim)
    ins = (x2d, lw['norm1_g'], lw['w_proj'], lw['fox_qg'], lw['fox_kg'], lw['fox_fb'], lw['head_ones'])
    out_w = (2 * W, D_RW_IN, W, W, W, W, W, W, W, W, W, W, LANE)
    out_dt = (F32, F32, BF16, F32, F32, BF16, BF16, BF16, F32, F32, BF16, BF16, F32)
    return pl.pallas_call(
        _proj_kernel,
        grid=(n // tm,),
        in_specs=[row(D_MODEL)] + [full(a) for a in ins[1:]],
        out_specs=[row(w) for w in out_w],
        out_shape=[jax.ShapeDtypeStruct((n, w), dt) for w, dt in zip(out_w, out_dt)],
        compiler_params=_cparams(("parallel",)),
        name="proj",
    )(*ins)


def _lru_kernel(xy_ref, conv0_ref, h0_ref, cw_ref, cb_ref, wa_ref, ba_ref, wi_ref, bi_ref, lam_ref,
                y_ref, convn_ref, hn_ref, tail_sc, h_sc, *, t_last):
    j = pl.program_id(1)
    W = BRANCH_W

    @pl.when(j == 0)
    def _():
        tail_sc[...] = conv0_ref[0]
        h_sc[...] = h0_ref[0]

    x = xy_ref[0, :, 0:W]
    y = xy_ref[0, :, W:2 * W]
    tt = x.shape[0]
    tail = tail_sc[...]
    row8 = _row_iota((8, W))
    cw = cw_ref[...]
    xc = cb_ref[...] + cw[CONV_W - 1:CONV_W, :] * x
    for kback in range(1, CONV_W):
        xs = pltpu.roll(x, kback, 0)
        head = jnp.where(row8 < kback, pltpu.roll(tail, kback, 0), xs[0:8])
        xs = head if tt == 8 else jnp.concatenate([head, xs[8:]], axis=0)
        xc = xc + cw[CONV_W - 1 - kback:CONV_W - kback, :] * xs
    r = _sigmoid(_dot(xc, wa_ref[...]) + ba_ref[...])
    ig = _sigmoid(_dot(xc, wi_ref[...]) + bi_ref[...])
    log_a = (-LRU_C) * r * _softplus(-lam_ref[...])
    a = jnp.exp(log_a)
    u = jnp.sqrt(-jnp.tanh(log_a) * (a * a + 1.0)) * (ig * xc)
    ap, hloc = _scan_rows(a, u)
    h = ap * h_sc[...] + hloc
    y_ref[0] = h * _gelu_tanh(y)
    h_sc[...] = h[tt - 1:tt, :]
    tail_sc[...] = x[tt - 8:tt, :]

    @pl.when(j == pl.num_programs(1) - 1)
    def _():
        hn_ref[0] = h[t_last - 1:t_last, :]
        convn_ref[0] = x[t_last - 8:t_last, :] if t_last >= 8 else jnp.where(
            row8 < 8 - t_last, pltpu.roll(tail, 8 - t_last, 0), pltpu.roll(x[0:8], 8 - t_last, 0))


def _lru(xy, conv0, h0, lw, tt, t_valid):
    b, tp, _ = xy.shape
    W = BRANCH_W
    nt = tp // tt
    t_last = t_valid - (nt - 1) * tt
    full = lambda a: pl.BlockSpec(a.shape, lambda i, j: (0,) * a.ndim)
    ws = (lw['lru_conv_w'], lw['lru_conv_b'], lw['lru_wa_bd'], lw['lru_ba'], lw['lru_wi_bd'], lw['lru_bi'], lw['lru_lambda'])
    return pl.pallas_call(
        functools.partial(_lru_kernel, t_last=t_last),
        grid=(b, nt),
        in_specs=[pl.BlockSpec((1, tt, 2 * W), lambda i, j: (i, j, 0)),
                  pl.BlockSpec((1, 8, W), lambda i, j: (i, 0, 0)),
                  pl.BlockSpec((1, 1, W), lambda i, j: (i, 0, 0))] + [full(a) for a in ws],
        out_specs=[pl.BlockSpec((1, tt, W), lambda i, j: (i, j, 0)),
                   pl.BlockSpec((1, 8, W), lambda i, j: (i, 0, 0)),
                   pl.BlockSpec((1, 1, W), lambda i, j: (i, 0, 0))],
        out_shape=[jax.ShapeDtypeStruct((b, tp, W), F32),
                   jax.ShapeDtypeStruct((b, 8, W), F32),
                   jax.ShapeDtypeStruct((b, 1, W), F32)],
        scratch_shapes=[pltpu.VMEM((8, W), F32), pltpu.VMEM((1, W), F32)],
        compiler_params=_cparams(("parallel", "arbitrary")),
        name="rglru",
    )(xy, conv0, h0, *ws)


def _block_diag(w):
    n, c, d = w.shape
    eye = jnp.eye(n, dtype=w.dtype)
    return (eye[:, None, :, None] * w[:, :, None, :]).reshape(n * c, n * d)


def _prep_layer(p, l):
    row = lambda a: a[l].reshape(1, -1).astype(F32)
    w_in = p['w_in'][l]
    o_f = N_MAIN
    o_rw = N_MAIN + N_HEADS
    o_g = o_rw + D_RW_IN
    w_f = jnp.pad(w_in[:, o_f:o_rw], ((0, 0), (0, LANE - N_HEADS)))
    lw = {
        'norm1_g': row(p['norm1_g']),
        'w_proj': jnp.concatenate([w_in[:, :o_f], w_in[:, o_rw:o_g], w_f], axis=1).astype(BF16),
        'w_gate': w_in[:, o_g:].astype(BF16),
        'fox_qg': jnp.tile(p['fox_qnorm_g'][l], N_HEADS).reshape(1, -1),
        'fox_kg': jnp.tile(p['fox_knorm_g'][l], N_HEADS).reshape(1, -1),
        'fox_fb': jnp.pad(p['fox_fbias'][l], (0, LANE - N_HEADS)).reshape(1, -1),
        'head_ones': _head_ones(),
        'lru_conv_w': p['lru_conv_w'][l],
        'lru_conv_b': row(p['lru_conv_b']),
        'lru_wa_bd': _block_diag(p['lru_wa'][l]).astype(BF16),
        'lru_ba': row(p['lru_ba']),
        'lru_wi_bd': _block_diag(p['lru_wi'][l]).astype(BF16),
        'lru_bi': row(p['lru_bi']),
        'lru_lambda': row(p['lru_lambda']),
        'rw_mu': row(p['rw_mu']),
        'rw_w0': row(p['rw_w0']),
        'rw_a0': row(p['rw_a0']),
        'rw_w2p': jnp.pad(p['rw_w2'][l], ((0, 96), (0, 0))).astype(BF16),
        'rw_a2p': jnp.pad(p['rw_a2'][l], ((32, 64), (0, 0))).astype(BF16),
        'rw_g2p': jnp.pad(p['rw_g2'][l], ((64, 0), (0, 0))).astype(BF16),
        'rw_kk': row(p['rw_kk']),
        'rw_ka': row(p['rw_ka']),
        'rw_rk': row(p['rw_rk']),
        'rw_ln_g': row(p['rw_ln_g']),
        'rw_ln_b': row(p['rw_ln_b']),
        'w_branch': p['w_branch'][l].astype(BF16),
        'w_out': p['w_out'][l].astype(BF16),
        'norm2_g': row(p['norm2_g']),
        'moe_w1': p['moe_w1'][l].astype(BF16),
        'moe_w3': p['moe_w3'][l].astype(BF16),
        'moe_w2': p['moe_w2'][l].astype(BF16),
    }
    w_r = jnp.concatenate([p['moe_we'][l], p['moe_wg'][l]], axis=1)
    lw['moe_wr'] = jnp.pad(w_r, ((0, 0), (0, LANE - w_r.shape[1])))
    b_r = jnp.concatenate([p['moe_be'][l], p['moe_bg'][l]])
    lw['moe_br'] = jnp.pad(b_r, (0, LANE - b_r.shape[0])).reshape(1, -1)
    return lw


def _rwkv_kernel(pd_ref, shift0_ref, s0_ref, mu_ref, w0_ref, w2_ref, a0_ref, a2_ref, g2_ref,
                 kkw_ref, ka_ref, rk_ref, lng_ref, lnb_ref, hs_ref,
                 y_ref, sn_ref, prev_sc, s_sc, *, t_valid):
    j = pl.program_id(1)
    W = BRANCH_W
    C = pd_ref.shape[1]

    @pl.when(j == 0)
    def _():
        prev_sc[...] = shift0_ref[0]
        s_sc[...] = s0_ref[0]

    pd = pd_ref[0]
    row = _row_iota((C, 1))
    prv = jnp.where(row == 0, prev_sc[...], pltpu.roll(pd, 1, 0))
    ps = pd + (prv - pd) * mu_ref[...]
    r = ps[:, 0:W]
    k = ps[:, W:2 * W]
    v = ps[:, 2 * W:3 * W]
    tail = ps[:, 3 * W:3 * W + RW_TAIL]
    w_log = -_softplus(-(w0_ref[...] + _dot(jnp.tanh(tail), w2_ref[...]))) - 0.5
    logw = -jnp.exp(w_log)
    a = _sigmoid(a0_ref[...] + _dot(tail, a2_ref[...]))
    g = _dot(_sigmoid(tail), g2_ref[...])
    hs = hs_ref[...]
    kk = k * kkw_ref[...]
    kk = kk / jnp.maximum(jnp.sqrt(_dot2(kk * kk, hs)), 1e-12)
    k = k * (1.0 + (a - 1.0) * ka_ref[...])
    live = (row + j * C) < t_valid
    logw = jnp.where(live, logw, 0.0)
    kk = jnp.where(live, kk, 0.0)
    kl = jnp.where(live, k, 0.0)

    c = _cumsum_rows(logw)
    e_pos = jnp.exp(c)
    e_neg = jnp.exp(-c)
    At = -kk * jnp.exp(c - logw)
    Bt = kk * a * e_neg
    Kt = kl * e_neg
    Rt = r * e_pos
    e_end = e_pos[C - 1:C, :]
    Bg = Bt * e_end
    Kg = Kt * e_end

    rr = _row_iota((C, C))
    cc = _col_iota((C, C))
    strict = cc < rr
    incl = cc <= rr
    ys = []
    for h in range(N_HEADS):
        sl = slice(h * HEAD_DIM, (h + 1) * HEAD_DIM)
        S = s_sc[h]
        vh = v[:, sl]
        AR = jnp.concatenate([At[:, sl], Rt[:, sl]], axis=0)
        BK = jnp.concatenate([Bt[:, sl], Kt[:, sl]], axis=0)
        M = _dot_nt(AR, BK)
        ARS = _dot_nt(AR, S)
        L = jnp.where(strict, M[0:C, 0:C], 0.0)
        X = ARS[0:C] + _dot(jnp.where(strict, M[0:C, C:2 * C], 0.0), vh)
        s = 1
        while s < C:
            X = X + _dot(L, X)
            s *= 2
            if s < C:
                L = _dot(L, L)
        yh = ARS[C:2 * C] + _dot(jnp.where(incl, M[C:2 * C, 0:C], 0.0), X) + _dot(jnp.where(incl, M[C:2 * C, C:2 * C], 0.0), vh)
        ys.append(yh)
        s_sc[h] = S * e_end[:, sl] + _dot_tn(X, Bg[:, sl]) + _dot_tn(vh, Kg[:, sl])
    y = jnp.concatenate(ys, axis=1)
    inv_d = 1.0 / HEAD_DIM
    mean = _dot2(y, hs) * inv_d
    dv = y - mean
    var = _dot2(dv * dv, hs) * inv_d
    yn = dv * lax.rsqrt(var + RW_GN_EPS) * lng_ref[...] + lnb_ref[...]
    bonus = _dot2(r * k * rk_ref[...], hs) * v
    y_ref[0] = (yn + bonus) * g
    prev_sc[...] = pd[C - 1:C, :]

    @pl.when(j == pl.num_programs(1) - 1)
    def _():
        sn_ref[0] = s_sc[...]


def _rwkv(pd, shift0, s0, lw, t_valid):
    b, tp, _ = pd.shape
    W = BRANCH_W
    C = RW_CHUNK
    full = lambda a: pl.BlockSpec(a.shape, lambda i, j: (0,) * a.ndim)
    ws = (lw['rw_mu'], lw['rw_w0'], lw['rw_w2p'], lw['rw_a0'], lw['rw_a2p'], lw['rw_g2p'],
          lw['rw_kk'], lw['rw_ka'], lw['rw_rk'], lw['rw_ln_g'], lw['rw_ln_b'], lw['head_ones'])
    return pl.pallas_call(
        functools.partial(_rwkv_kernel, t_valid=t_valid),
        grid=(b, tp // C),
        in_specs=[pl.BlockSpec((1, C, D_RW_IN), lambda i, j: (i, j, 0)),
                  pl.BlockSpec((1, 1, D_RW_IN), lambda i, j: (i, 0, 0)),
                  pl.BlockSpec((1, N_HEADS, HEAD_DIM, HEAD_DIM), lambda i, j: (i, 0, 0, 0))] + [full(a) for a in ws],
        out_specs=[pl.BlockSpec((1, C, W), lambda i, j: (i, j, 0)),
                   pl.BlockSpec((1, N_HEADS, HEAD_DIM, HEAD_DIM), lambda i, j: (i, 0, 0, 0))],
        out_shape=[jax.ShapeDtypeStruct((b, tp, W), F32),
                   jax.ShapeDtypeStruct((b, N_HEADS, HEAD_DIM, HEAD_DIM), F32)],
        scratch_shapes=[pltpu.VMEM((1, D_RW_IN), F32), pltpu.VMEM((N_HEADS, HEAD_DIM, HEAD_DIM), F32)],
        compiler_params=_cparams(("parallel", "arbitrary")),
        name="rwkv7",
    )(pd, shift0, s0, *ws)


ATT_BLK = 256


def _suffix_ones(n):
    i = jnp.arange(n)
    return jnp.concatenate([(i[:, None] > i[None, :]), jnp.ones((n, n), bool)], axis=1).astype(BF16)


def _sb_prompt_kernel(q_ref, k_ref, v_ref, m_ref, o_ref):
    i = pl.program_id(1)
    T = ATT_BLK
    msuf = m_ref[...]
    lower = _col_iota((T, T)) < _row_iota((T, T))
    outs = []
    for h in range(N_HEADS):
        sl = slice(h * HEAD_DIM, (h + 1) * HEAD_DIM)
        qh = q_ref[0, :, sl]

        def block(kj, run, acc, diag):
            start = pl.multiple_of(kj * T, T)
            z = _dot_nt(qh, k_ref[0, pl.ds(start, T), sl])
            ls = -_softplus(z)
            if diag:
                ls = jnp.where(lower, ls, 0.0)
            st = _dot2(ls, msuf)
            w = jnp.exp(z + ls + st[:, 0:T] + run)
            if diag:
                w = jnp.where(lower, w, 0.0)
            acc = acc + _dot(w, v_ref[0, pl.ds(start, T), sl])
            return run + st[:, T:2 * T], acc

        run, acc = block(i, jnp.zeros((T, T), F32), jnp.zeros((T, HEAD_DIM), F32), True)

        def body(jj, carry):
            return block(i - 1 - jj, carry[0], carry[1], False)

        run, acc = lax.fori_loop(0, i, body, (run, acc))
        outs.append(acc)
    o_ref[0] = jnp.concatenate(outs, axis=1)


def _sb_prompt(q, k, v):
    b, t, W = q.shape
    msuf = _suffix_ones(ATT_BLK)
    return pl.pallas_call(
        _sb_prompt_kernel,
        grid=(b, t // ATT_BLK),
        in_specs=[pl.BlockSpec((1, ATT_BLK, W), lambda i, j: (i, j, 0)),
                  pl.BlockSpec((1, t, W), lambda i, j: (i, 0, 0)),
                  pl.BlockSpec((1, t, W), lambda i, j: (i, 0, 0)),
                  pl.BlockSpec(msuf.shape, lambda i, j: (0, 0))],
        out_specs=pl.BlockSpec((1, ATT_BLK, W), lambda i, j: (i, j, 0)),
        out_shape=jax.ShapeDtypeStruct((b, t, W), F32),
        compiler_params=_cparams(("parallel", "arbitrary")),
        name="sb_prompt",
    )(q, k, v, msuf)


def _fox_prompt_kernel(q_ref, k_ref, v_ref, fq_ref, fk_ref, o_ref):
    i = pl.program_id(1)
    T = ATT_BLK
    lower = _col_iota((T, T)) <= _row_iota((T, T))
    outs = []
    for h in range(N_HEADS):
        sl = slice(h * HEAD_DIM, (h + 1) * HEAD_DIM)
        qh = q_ref[0, :, sl]
        fq = fq_ref[0, :, h:h + 1]

        def block(kj, m, l, acc, diag):
            start = pl.multiple_of(kj * T, T)
            s = _dot_nt(qh, k_ref[0, pl.ds(start, T), sl]) + (fq - fk_ref[0, h:h + 1, pl.ds(start, T)])
            if diag:
                s = jnp.where(lower, s, -jnp.inf)
            m_new = jnp.maximum(m, jnp.max(s, axis=1, keepdims=True))
            alpha = jnp.exp(m - m_new)
            p = jnp.exp(s - m_new)
            l = l * alpha + jnp.sum(p, axis=1, keepdims=True)
            acc = acc * alpha + _dot(p, v_ref[0, pl.ds(start, T), sl])
            return m_new, l, acc

        m, l, acc = block(i, jnp.full((T, 1), -jnp.inf, F32), jnp.zeros((T, 1), F32), jnp.zeros((T, HEAD_DIM), F32), True)

        def body(jj, carry):
            return block(i - 1 - jj, *carry, False)

        m, l, acc = lax.fori_loop(0, i, body, (m, l, acc))
        outs.append(acc / l)
    o_ref[0] = jnp.concatenate(outs, axis=1)


def _fox_prompt(q, k, v, f_col, f_row):
    b, t, W = q.shape
    return pl.pallas_call(
        _fox_prompt_kernel,
        grid=(b, t // ATT_BLK),
        in_specs=[pl.BlockSpec((1, ATT_BLK, W), lambda i, j: (i, j, 0)),
                  pl.BlockSpec((1, t, W), lambda i, j: (i, 0, 0)),
                  pl.BlockSpec((1, t, W), lambda i, j: (i, 0, 0)),
                  pl.BlockSpec((1, ATT_BLK, 8), lambda i, j: (i, j, 0)),
                  pl.BlockSpec((1, 8, t), lambda i, j: (i, 0, 0))],
        out_specs=pl.BlockSpec((1, ATT_BLK, W), lambda i, j: (i, j, 0)),
        out_shape=jax.ShapeDtypeStruct((b, t, W), F32),
        compiler_params=_cparams(("parallel", "arbitrary")),
        name="fox_prompt",
    )(q, k, v, f_col, f_row)


def _cumsum_kernel(x_ref, o_ref):
    n = x_ref.shape[2] // LANE

    def body(c, carry):
        start = pl.multiple_of(c * LANE, LANE)
        s = _cumsum_lanes(x_ref[0, :, pl.ds(start, LANE)]) + carry
        o_ref[0, :, pl.ds(start, LANE)] = s
        return jnp.broadcast_to(s[:, LANE - 1:LANE], s.shape)

    lax.fori_loop(0, n, body, jnp.zeros((8, LANE), F32))


def _cumsum_prompt(logf_row):
    b, _, t = logf_row.shape
    return pl.pallas_call(
        _cumsum_kernel,
        grid=(b,),
        in_specs=[pl.BlockSpec((1, 8, t), lambda i: (i, 0, 0))],
        out_specs=pl.BlockSpec((1, 8, t), lambda i: (i, 0, 0)),
        out_shape=jax.ShapeDtypeStruct((b, 8, t), F32),
        compiler_params=_cparams(("parallel",)),
        name="logf_cumsum",
    )(logf_row)


PAGES_PER_STEP = 8


def _sb_sample_kernel(pt_ref, q_ref, kn_ref, vn_ref, m_ref, *rest, n_new):
    G = PAGES_PER_STEP
    k_refs, v_refs = rest[0:G], rest[G:2 * G]
    o_ref, run_sc, acc_sc = rest[2 * G:]
    j = pl.program_id(1)
    msuf = m_ref[...]

    def process(k_at, v_at, mask):
        z = jnp.concatenate([_dot(q_ref[0, h], k_at(h)) for h in range(N_HEADS)], axis=0)
        ls = -_softplus(z)
        if mask is not None:
            ls = jnp.where(mask, ls, 0.0)
        st = _dot2(ls, msuf)
        w = jnp.exp(z + ls + st[:, 0:LANE] + run_sc[...])
        if mask is not None:
            w = jnp.where(mask, w, 0.0)
        for h in range(N_HEADS):
            acc_sc[h] += _dot_nt(w[h * 8:(h + 1) * 8], v_at(h))
        run_sc[...] += st[:, LANE:2 * LANE]

    @pl.when(j == 0)
    def _():
        run_sc[...] = jnp.zeros_like(run_sc)
        acc_sc[...] = jnp.zeros_like(acc_sc)
        shape = (N_HEADS * 8, LANE)
        t = _row_iota(shape) % 8
        s = _col_iota(shape)
        process(lambda h: kn_ref[0, h], lambda h: vn_ref[0, h], (s < t) & (s < n_new))

    for g in range(G):
        process(lambda h: k_refs[g][0, 0, h], lambda h: v_refs[g][0, 0, h], None)

    @pl.when(j == pl.num_programs(1) - 1)
    def _():
        o_ref[0] = acc_sc[...]


def _sb_sample(q, kn_t, vn_t, cache_kt, cache_vt, page_table, layer, n_new):
    b = q.shape[0]
    n_pages = page_table.shape[1]
    G = PAGES_PER_STEP
    msuf = _suffix_ones(LANE)

    def page_spec(g):
        return pl.BlockSpec((1, 1, N_HEADS, HEAD_DIM, PAGE),
                            lambda i, j, pt: (layer, pt[i, n_pages - 1 - (j * G + g)], 0, 0, 0))

    new_spec = pl.BlockSpec((1, N_HEADS, HEAD_DIM, LANE), lambda i, j, pt: (i, 0, 0, 0))
    out_spec = pl.BlockSpec((1, N_HEADS, 8, HEAD_DIM), lambda i, j, pt: (i, 0, 0, 0))
    return pl.pallas_call(
        functools.partial(_sb_sample_kernel, n_new=n_new),
        grid_spec=pltpu.PrefetchScalarGridSpec(
            num_scalar_prefetch=1,
            grid=(b, n_pages // G),
            in_specs=[out_spec, new_spec, new_spec, pl.BlockSpec(msuf.shape, lambda i, j, pt: (0, 0))]
                     + [page_spec(g) for g in range(G)] * 2,
            out_specs=out_spec,
            scratch_shapes=[pltpu.VMEM((N_HEADS * 8, LANE), F32), pltpu.VMEM((N_HEADS, 8, HEAD_DIM), F32)]),
        out_shape=jax.ShapeDtypeStruct((b, N_HEADS, 8, HEAD_DIM), F32),
        compiler_params=_cparams(("parallel", "arbitrary")),
        name="sb_sample",
    )(page_table, q, kn_t, vn_t, msuf, *([cache_kt] * G), *([cache_vt] * G))


def _fox_sample_kernel(pt_ref, q_ref, fq_ref, kn_ref, vn_ref, f_ref, fn_ref, *rest, n_new):
    G = PAGES_PER_STEP
    k_refs, v_refs = rest[0:G], rest[G:2 * G]
    o_ref, m_sc, l_sc, acc_sc = rest[2 * G:]
    j = pl.program_id(1)

    def process(h, k_list, v_list, f_row, mask):
        s = jnp.concatenate([_dot(q_ref[0, h], kt) for kt in k_list], axis=1)
        s = s + (jnp.concatenate([fq_ref[0, h]] * len(k_list), axis=1) - f_row)
        if mask is not None:
            s = jnp.where(mask, s, -jnp.inf)
        m_old = m_sc[h]
        m_new = jnp.maximum(m_old, jnp.max(s, axis=1, keepdims=True))
        alpha = jnp.exp(m_old - m_new)
        p = jnp.exp(s - m_new)
        l_sc[h] = l_sc[h] * alpha + jnp.sum(p, axis=1, keepdims=True)
        pv = _dot_nt(p[:, 0:LANE], v_list[0])
        for g in range(1, len(v_list)):
            pv = pv + _dot_nt(p[:, g * LANE:(g + 1) * LANE], v_list[g])
        acc_sc[h] = acc_sc[h] * alpha + pv
        m_sc[h] = m_new

    @pl.when(j == 0)
    def _():
        m_sc[...] = jnp.full_like(m_sc, -1e30)
        l_sc[...] = jnp.zeros_like(l_sc)
        acc_sc[...] = jnp.zeros_like(acc_sc)
        t = _row_iota((8, LANE))
        s = _col_iota((8, LANE))
        for h in range(N_HEADS):
            process(h, [kn_ref[0, h]], [vn_ref[0, h]], fn_ref[0, h:h + 1, :], (s <= t) & (s < n_new))

    for h in range(N_HEADS):
        process(h, [k_refs[g][0, 0, h] for g in range(G)], [v_refs[g][0, 0, h] for g in range(G)],
                f_ref[0, h:h + 1, :], None)

    @pl.when(j == pl.num_programs(1) - 1)
    def _():
        o_ref[0] = acc_sc[...] / l_sc[...]


def _fox_sample(q, fq, kn_t, vn_t, f_all, cache_kt, cache_vt, page_table, layer, n_new):
    b = q.shape[0]
    n_pages = page_table.shape[1]
    G = PAGES_PER_STEP

    def page_spec(g):
        return pl.BlockSpec((1, 1, N_HEADS, HEAD_DIM, PAGE), lambda i, j, pt: (layer, pt[i, j * G + g], 0, 0, 0))

    new_spec = pl.BlockSpec((1, N_HEADS, HEAD_DIM, LANE), lambda i, j, pt: (i, 0, 0, 0))
    out_spec = pl.BlockSpec((1, N_HEADS, 8, HEAD_DIM), lambda i, j, pt: (i, 0, 0, 0))
    return pl.pallas_call(
        functools.partial(_fox_sample_kernel, n_new=n_new),
        grid_spec=pltpu.PrefetchScalarGridSpec(
            num_scalar_prefetch=1,
            grid=(b, n_pages // G),
            in_specs=[out_spec,
                      pl.BlockSpec((1, N_HEADS, 8, LANE), lambda i, j, pt: (i, 0, 0, 0)),
                      new_spec, new_spec,
                      pl.BlockSpec((1, 8, G * PAGE), lambda i, j, pt: (i, 0, j)),
                      pl.BlockSpec((1, 8, LANE), lambda i, j, pt: (i, 0, n_pages))]
                     + [page_spec(g) for g in range(G)] * 2,
            out_specs=out_spec,
            scratch_shapes=[pltpu.VMEM((N_HEADS, 8, 1), F32), pltpu.VMEM((N_HEADS, 8, 1), F32),
                            pltpu.VMEM((N_HEADS, 8, HEAD_DIM), F32)]),
        out_shape=jax.ShapeDtypeStruct((b, N_HEADS, 8, HEAD_DIM), F32),
        compiler_params=_cparams(("parallel", "arbitrary")),
        name="fox_sample",
    )(page_table, q, fq, kn_t, vn_t, f_all, f_all, *([cache_kt] * G), *([cache_vt] * G))


def _f_sample_kernel(pt_ref, pool_ref, new_ref, o_ref):
    i = pl.program_id(0)
    n_pages = pt_ref.shape[1]

    def last_lane(s):
        return jnp.broadcast_to(s[:, LANE - 1:LANE], s.shape)

    def body(j, carry):
        p = pt_ref[i, j]
        tile = pool_ref[0, p // 2]
        s = _cumsum_lanes(jnp.where(p % 2 == 1, pltpu.roll(tile, N_HEADS, 0), tile)) + carry
        o_ref[0, :, pl.ds(pl.multiple_of(j * LANE, LANE), LANE)] = s
        return last_lane(s)

    carry = lax.fori_loop(0, n_pages, body, jnp.zeros((8, LANE), F32))
    o_ref[0, :, n_pages * LANE:(n_pages + 1) * LANE] = _cumsum_lanes(new_ref[0]) + carry


def _f_sample(pool_pairs, logf_new, page_table, layer):
    b, n_pages = page_table.shape
    width = (n_pages + 1) * LANE
    return pl.pallas_call(
        _f_sample_kernel,
        grid_spec=pltpu.PrefetchScalarGridSpec(
            num_scalar_prefetch=1,
            grid=(b,),
            in_specs=[pl.BlockSpec((1,) + pool_pairs.shape[1:], lambda i, pt: (layer, 0, 0, 0)),
                      pl.BlockSpec((1, 8, LANE), lambda i, pt: (i, 0, 0))],
            out_specs=pl.BlockSpec((1, 8, width), lambda i, pt: (i, 0, 0))),
        out_shape=jax.ShapeDtypeStruct((b, 8, width), F32),
        compiler_params=_cparams(("arbitrary",)),
        name="logf_cumsum_paged",
    )(page_table, pool_pairs, logf_new)


def _pages_t(cache):
    return jnp.transpose(cache, (0, 1, 3, 4, 2))


def _logf_pairs(cache_logf):
    l, n, p, h = cache_logf.shape
    return jnp.swapaxes(cache_logf, 2, 3).reshape(l, n // 2, 2 * h, p)


def _heads_t(x):
    b, t, _ = x.shape
    xt = jnp.transpose(x.reshape(b, t, N_HEADS, HEAD_DIM), (0, 2, 3, 1))
    return jnp.pad(xt, ((0, 0), (0, 0), (0, 0), (0, LANE - t)))


def _heads_q(q):
    b, t, _ = q.shape
    qh = jnp.transpose(q.reshape(b, t, N_HEADS, HEAD_DIM), (0, 2, 1, 3))
    return jnp.pad(qh, ((0, 0), (0, 0), (0, 8 - t), (0, 0))).astype(BF16)


def _heads_out(o, t):
    b = o.shape[0]
    return jnp.transpose(o[:, :, :t, :], (0, 2, 1, 3)).reshape(b, t, BRANCH_W)


def _sample_sb(q, k_new, v_new, cache_kt, cache_vt, page_table, layer):
    t = q.shape[1]
    o = _sb_sample(_heads_q(q), _heads_t(k_new), _heads_t(v_new), cache_kt, cache_vt, page_table, layer, t)
    return _heads_out(o, t)


def _sample_fox(q, k_new, v_new, logf_new, cache_kt, cache_vt, pool_pairs, page_table, layer):
    b, t, _ = q.shape
    n_pages = page_table.shape[1]
    lf = jnp.pad(jnp.swapaxes(logf_new, 1, 2), ((0, 0), (0, 8 - N_HEADS), (0, LANE - t)))
    f_all = _f_sample(pool_pairs, lf, page_table, layer)
    f_q = f_all[:, :N_HEADS, n_pages * PAGE:n_pages * PAGE + 8]
    f_q = jnp.broadcast_to(f_q[..., None], (b, N_HEADS, 8, LANE))
    o = _fox_sample(_heads_q(q), f_q, _heads_t(k_new), _heads_t(v_new), f_all, cache_kt, cache_vt,
                    page_table, layer, t)
    return _heads_out(o, t), f_all


def _merge_kernel(x_ref, ya_ref, yb_ref, yc_ref, yd_ref, g_ref, wg_ref, wb_ref, wo_ref, o_ref):
    x = x_ref[...]
    xn = _rmsnorm(x, g_ref[...]).astype(BF16)
    m = None
    for n, y_ref in enumerate((ya_ref, yb_ref, yc_ref, yd_ref)):
        gate = _sigmoid(jnp.dot(xn, wg_ref[:, n * D_MODEL:(n + 1) * D_MODEL], preferred_element_type=F32))
        term = gate * _dot(y_ref[...], wb_ref[n])
        m = term if m is None else m + term
    o_ref[...] = x + _dot(m, wo_ref[...])


def _merge(x2d, ys, lw, tm):
    n = x2d.shape[0]
    row = lambda w: pl.BlockSpec((tm, w), lambda i: (i, 0))
    full = lambda a: pl.BlockSpec(a.shape, lambda i: (0,) * a.ndim)
    ws = (lw['norm1_g'], lw['w_gate'], lw['w_branch'], lw['w_out'])
    return pl.pallas_call(
        _merge_kernel,
        grid=(n // tm,),
        in_specs=[row(D_MODEL)] + [row(BRANCH_W)] * 4 + [full(a) for a in ws],
        out_specs=row(D_MODEL),
        out_shape=jax.ShapeDtypeStruct((n, D_MODEL), F32),
        compiler_params=_cparams(("parallel",)),
        name="merge",
    )(x2d, *ys, *ws)


def _router_gate(logits):
    lane = _col_iota(logits.shape)
    lane_f = lane.astype(F32)
    big = float(LANE)
    neg = -jnp.inf
    gl = jnp.where((lane >= N_EXPERTS) & (lane < N_EXPERTS + N_GROUPS), logits, neg)
    g_max = jnp.max(gl, axis=1, keepdims=True)
    g_idx = jnp.min(jnp.where(gl == g_max, lane_f, big), axis=1, keepdims=True) - float(N_EXPERTS)
    g_w = 1.0 / jnp.sum(jnp.exp(gl - g_max), axis=1, keepdims=True)
    per_group = N_EXPERTS // N_GROUPS
    lo = g_idx * float(per_group)
    el = jnp.where((lane_f >= lo) & (lane_f < lo + float(per_group)), logits, neg)
    m1 = jnp.max(el, axis=1, keepdims=True)
    i1 = jnp.min(jnp.where(el == m1, lane_f, big), axis=1, keepdims=True)
    el2 = jnp.where(lane_f == i1, neg, el)
    m2 = jnp.max(el2, axis=1, keepdims=True)
    i2 = jnp.min(jnp.where(el2 == m2, lane_f, big), axis=1, keepdims=True)
    d = jnp.exp(m2 - m1)
    w1 = g_w / (1.0 + d)
    return jnp.where(lane_f == i1, w1, 0.0) + jnp.where(lane_f == i2, w1 * d, 0.0)


def _moe_kernel(x_ref, g_ref, wr_ref, br_ref, w1_ref, w3_ref, w2_ref, o_ref, xn_sc, gate_sc, acc_sc):
    e = pl.program_id(1)

    @pl.when(e == 0)
    def _():
        x = x_ref[...]
        xn = _rmsnorm(x, g_ref[...])
        xn_sc[...] = xn.astype(BF16)
        logits = jnp.dot(xn, wr_ref[...], preferred_element_type=F32, precision=lax.Precision.HIGHEST) + br_ref[...]
        gate_sc[...] = _router_gate(logits)
        acc_sc[...] = x

    xn = xn_sc[...]
    a = jnp.dot(xn, w1_ref[0], preferred_element_type=F32)
    h = a * _sigmoid(a) * jnp.dot(xn, w3_ref[0], preferred_element_type=F32)
    ge = jnp.sum(jnp.where(_col_iota(gate_sc.shape) == e, gate_sc[...], 0.0), axis=1, keepdims=True)
    acc_sc[...] += _dot(h * ge, w2_ref[0])

    @pl.when(e == pl.num_programs(1) - 1)
    def _():
        o_ref[...] = acc_sc[...]


def _moe(x2d, lw, tm):
    n = x2d.shape[0]
    row = pl.BlockSpec((tm, D_MODEL), lambda i, e: (i, 0))
    full = lambda a: pl.BlockSpec(a.shape, lambda i, e: (0,) * a.ndim)
    return pl.pallas_call(
        _moe_kernel,
        grid=(n // tm, N_EXPERTS),
        in_specs=[row, full(lw['norm2_g']), full(lw['moe_wr']), full(lw['moe_br']),
                  pl.BlockSpec((1, D_MODEL, D_EXP), lambda i, e: (e, 0, 0)),
                  pl.BlockSpec((1, D_MODEL, D_EXP), lambda i, e: (e, 0, 0)),
                  pl.BlockSpec((1, D_EXP, D_MODEL), lambda i, e: (e, 0, 0))],
        out_specs=row,
        out_shape=jax.ShapeDtypeStruct((n, D_MODEL), F32),
        scratch_shapes=[pltpu.VMEM((tm, D_MODEL), BF16), pltpu.VMEM((tm, LANE), F32), pltpu.VMEM((tm, D_MODEL), F32)],
        compiler_params=_cparams(("parallel", "arbitrary")),
        name="moe",
    )(x2d, lw['norm2_g'], lw['moe_wr'], lw['moe_br'], lw['moe_w1'], lw['moe_w3'], lw['moe_w2'])


def _pad_rows(a, mult):
    t = a.shape[1]
    tp = -(-t // mult) * mult
    return a if tp == t else jnp.pad(a, ((0, 0), (0, tp - t), (0, 0)))


def _layer(x, lw, state, paged):
    b, t, _ = x.shape
    n = b * t
    W = BRANCH_W
    conv0, h0, shift0, wkv0 = state
    tm = min(n, 256)
    x2d = x.reshape(n, D_MODEL)
    (lru_xy, rw_in, sbq, sbk, sbv, sbkb, sbvb, fq, fk, fv, fkb, fvb, logf_pad) = _proj(x2d, lw, tm)
    r3 = lambda a: a.reshape(b, t, a.shape[-1])
    logf = r3(logf_pad[:, :N_HEADS])

    tt = 256 if t % 256 == 0 else 8
    y_a, conv_n, h_n = _lru(_pad_rows(r3(lru_xy), tt), jnp.pad(conv0, ((0, 0), (8 - (CONV_W - 1), 0), (0, 0))),
                            h0[:, None, :], lw, tt, t)
    rw3 = r3(rw_in)
    y_d, wkv_n = _rwkv(_pad_rows(rw3, RW_CHUNK), shift0[:, None, :], wkv0, lw, t)

    if paged is None:
        y_b = _sb_prompt(r3(sbq), r3(sbkb), r3(sbvb))
        f_row = _cumsum_prompt(jnp.pad(jnp.swapaxes(logf, 1, 2), ((0, 0), (0, 8 - N_HEADS), (0, 0))))
        y_c = _fox_prompt(r3(fq), r3(fkb), r3(fvb), jnp.swapaxes(f_row, 1, 2), f_row)
    else:
        sb_kt, sb_vt, fox_kt, fox_vt, logf_pairs, page_table, layer = paged
        y_b = _sample_sb(r3(sbq), r3(sbk), r3(sbv), sb_kt, sb_vt, page_table, layer)
        y_c, _ = _sample_fox(r3(fq), r3(fk), r3(fv), logf, fox_kt, fox_vt, logf_pairs, page_table, layer)

    ys = (y_a[:, :t].reshape(n, W), y_b.reshape(n, W), y_c.reshape(n, W), y_d[:, :t].reshape(n, W))
    x1 = _merge(x2d, ys, lw, tm)
    x2 = _moe(x1, lw, min(n, 512))
    hd = lambda a: a.reshape(b, t, N_HEADS, HEAD_DIM)
    new_state = (hd(sbk), hd(sbv), hd(fk), hd(fv), logf, conv_n[:, 8 - (CONV_W - 1):], h_n[:, 0], rw3[:, t - 1], wkv_n)
    return x2.reshape(b, t, D_MODEL), new_state


def kernel(x_prompt, x_sample, cache_sb_k, cache_sb_v, cache_fox_k, cache_fox_v, cache_fox_logf, state_lru_conv, state_lru_h, state_rwkv_shift, state_rwkv_wkv, page_table, norm1_g, w_in, lru_conv_w, lru_conv_b, lru_wa, lru_ba, lru_wi, lru_bi, lru_lambda, fox_qnorm_g, fox_knorm_g, fox_fbias, rw_mu, rw_w0, rw_w2, rw_a0, rw_a2, rw_g2, rw_kk, rw_ka, rw_rk, rw_ln_g, rw_ln_b, w_branch, w_out, norm2_g, moe_wg, moe_bg, moe_we, moe_be, moe_w1, moe_w3, moe_w2):
    params = dict(norm1_g=norm1_g, w_in=w_in, lru_conv_w=lru_conv_w, lru_conv_b=lru_conv_b, lru_wa=lru_wa,
                  lru_ba=lru_ba, lru_wi=lru_wi, lru_bi=lru_bi, lru_lambda=lru_lambda, fox_qnorm_g=fox_qnorm_g,
                  fox_knorm_g=fox_knorm_g, fox_fbias=fox_fbias, rw_mu=rw_mu, rw_w0=rw_w0, rw_w2=rw_w2, rw_a0=rw_a0,
                  rw_a2=rw_a2, rw_g2=rw_g2, rw_kk=rw_kk, rw_ka=rw_ka, rw_rk=rw_rk, rw_ln_g=rw_ln_g, rw_ln_b=rw_ln_b,
                  w_branch=w_branch, w_out=w_out, norm2_g=norm2_g, moe_wg=moe_wg, moe_bg=moe_bg, moe_we=moe_we,
                  moe_be=moe_be, moe_w1=moe_w1, moe_w3=moe_w3, moe_w2=moe_w2)
    depth = w_in.shape[0]
    bp = x_prompt.shape[0]
    sb_kt, sb_vt = _pages_t(cache_sb_k), _pages_t(cache_sb_v)
    fox_kt, fox_vt = _pages_t(cache_fox_k), _pages_t(cache_fox_v)
    logf_pairs = _logf_pairs(cache_fox_logf)
    prompt_init = (jnp.zeros((bp, CONV_W - 1, BRANCH_W), F32), jnp.zeros((bp, BRANCH_W), F32),
                   jnp.zeros((bp, D_RW_IN), F32), jnp.zeros((bp, N_HEADS, HEAD_DIM, HEAD_DIM), F32))
    y_p, y_s = x_prompt, x_sample
    st_p, st_s = [], []
    for l in range(depth):
        lw = _prep_layer(params, l)
        y_p, sp = _layer(y_p, lw, prompt_init, None)
        y_s, ss = _layer(y_s, lw, (state_lru_conv[l], state_lru_h[l], state_rwkv_shift[l], state_rwkv_wkv[l]),
                         (sb_kt, sb_vt, fox_kt, fox_vt, logf_pairs, page_table, l))
        st_p.append(sp)
        st_s.append(ss)
    stack = lambda sts: [jnp.stack([s[i] for s in sts], axis=0) for i in range(9)]
    return (y_p, y_s, *stack(st_p), *stack(st_s))
```

```python
import functools
import math

import jax
import jax.numpy as jnp
from jax import lax
from jax.experimental import pallas as pl
from jax.experimental.pallas import tpu as pltpu

F32 = jnp.float32
BF16 = jnp.bfloat16

D_MODEL = 1024
HEAD_DIM = 64
N_HEADS = 4
BRANCH_W = N_HEADS * HEAD_DIM
CONV_W = 4
LRU_C = 8.0
PAGE = 128
RW_TAIL = 128
D_RW_IN = 3 * BRANCH_W + RW_TAIL
N_MAIN = 8 * BRANCH_W
N_GROUPS = 4
N_EXPERTS = 16
D_EXP = 256
RMS_EPS = 1e-6
RW_GN_EPS = 64e-5
ATT_SCALE = HEAD_DIM ** -0.5
LANE = 128
RW_CHUNK = 64
VMEM_LIMIT = 56 * 1024 * 1024


def _cparams(sem):
    return pltpu.CompilerParams(dimension_semantics=sem, vmem_limit_bytes=VMEM_LIMIT)


def _dot(a, b):
    return jnp.dot(a.astype(BF16), b.astype(BF16), preferred_element_type=F32)


def _dot_nt(a, b):
    return lax.dot_general(a.astype(BF16), b.astype(BF16), (((1,), (1,)), ((), ())), preferred_element_type=F32)


def _dot_tn(a, b):
    return lax.dot_general(a.astype(BF16), b.astype(BF16), (((0,), (0,)), ((), ())), preferred_element_type=F32)


def _dot2(a, b_bf16):
    hi = a.astype(BF16)
    lo = (a - hi.astype(F32)).astype(BF16)
    return (jnp.dot(hi, b_bf16, preferred_element_type=F32) + jnp.dot(lo, b_bf16, preferred_element_type=F32))


def _softplus(x):
    return jnp.maximum(x, 0.0) + jnp.log1p(jnp.exp(-jnp.abs(x)))


def _sigmoid(x):
    return 1.0 / (1.0 + jnp.exp(-x))


def _gelu_tanh(x):
    return 0.5 * x * (1.0 + jnp.tanh(math.sqrt(2.0 / math.pi) * (x + 0.044715 * (x * x * x))))


def _rmsnorm(x, g):
    return x * lax.rsqrt(jnp.mean(x * x, axis=-1, keepdims=True) + RMS_EPS) * g


def _row_iota(shape):
    return lax.broadcasted_iota(jnp.int32, shape, 0)


def _col_iota(shape):
    return lax.broadcasted_iota(jnp.int32, shape, 1)


def _scan_rows(a, u):
    n = a.shape[0]
    row = _row_iota(a.shape)
    s = 1
    while s < n:
        valid = row >= s
        a_sh = pltpu.roll(a, s, 0)
        u_sh = pltpu.roll(u, s, 0)
        u = jnp.where(valid, a * u_sh + u, u)
        a = jnp.where(valid, a * a_sh, a)
        s *= 2
    return a, u


def _cumsum_rows(x):
    n = x.shape[0]
    row = _row_iota(x.shape)
    s = 1
    while s < n:
        x = x + jnp.where(row >= s, pltpu.roll(x, s, 0), 0.0)
        s *= 2
    return x


def _cumsum_lanes(x):
    ax = x.ndim - 1
    n = x.shape[ax]
    col = lax.broadcasted_iota(jnp.int32, x.shape, ax)
    s = 1
    while s < n:
        x = x + jnp.where(col >= s, pltpu.roll(x, s, ax), 0.0)
        s *= 2
    return x


def _head_ones():
    i = jnp.arange(BRANCH_W) // HEAD_DIM
    return (i[:, None] == i[None, :]).astype(BF16)


def _proj_kernel(x_ref, g_ref, w_ref, qg_ref, kg_ref, fb_ref, hs_ref,
                 lru_ref, rw_ref, sbq_ref, sbk_ref, sbv_ref, sbkb_ref, sbvb_ref,
                 fq_ref, fk_ref, fv_ref, fkb_ref, fvb_ref, logf_ref):
    xn = _rmsnorm(x_ref[...], g_ref[...]).astype(BF16)

    def seg(a, b):
        return jnp.dot(xn, w_ref[:, a:b], preferred_element_type=F32)

    W = BRANCH_W
    lru_ref[...] = seg(0, 2 * W)
    sbq_ref[...] = (seg(2 * W, 3 * W) * ATT_SCALE).astype(BF16)
    k = seg(3 * W, 4 * W)
    sbk_ref[...] = k
    sbkb_ref[...] = k.astype(BF16)
    v = seg(4 * W, 5 * W)
    sbv_ref[...] = v
    sbvb_ref[...] = v.astype(BF16)
    hs = hs_ref[...]
    q = seg(5 * W, 6 * W)
    q = q * lax.rsqrt(_dot2(q * q, hs) * (1.0 / HEAD_DIM) + RMS_EPS) * qg_ref[...]
    fq_ref[...] = (q * ATT_SCALE).astype(BF16)
    k = seg(6 * W, 7 * W)
    k = k * lax.rsqrt(_dot2(k * k, hs) * (1.0 / HEAD_DIM) + RMS_EPS) * kg_ref[...]
    fk_ref[...] = k
    fkb_ref[...] = k.astype(BF16)
    v = seg(7 * W, 8 * W)
    fv_ref[...] = v
    fvb_ref[...] = v.astype(BF16)
    rw_ref[...] = seg(N_MAIN, N_MAIN + D_RW_IN)
    f = seg(N_MAIN + D_RW_IN, N_MAIN + D_RW_IN + LANE) + fb_ref[...]
    logf_ref[...] = -_softplus(-f)


def _proj(x2d, lw, tm):
    n = x2d.shape[0]
    W = BRANCH_W
    row = lambda w: pl.BlockSpec((tm, w), lambda i: (i, 0))
    full = lambda a: pl.BlockSpec(a.shape, lambda i: (0,) * a.ndim)
    ins = (x2d, lw['norm1_g'], lw['w_proj'], lw['fox_qg'], lw['fox_kg'], lw['fox_fb'], lw['head_ones'])
    out_w = (2 * W, D_RW_IN, W, W, W, W, W, W, W, W, W, W, LANE)
    out_dt = (F32, F32, BF16, F32, F32, BF16, BF16, BF16, F32, F32, BF16, BF16, F32)
    layer = lw['layer']
    w_spec = pl.BlockSpec((None, D_MODEL, N_PROJ), lambda i: (layer, 0, 0))
    return pl.pallas_call(
        _proj_kernel,
        grid=(n // tm,),
        in_specs=[row(D_MODEL), full(ins[1]), w_spec] + [full(a) for a in ins[3:]],
        out_specs=[row(w) for w in out_w],
        out_shape=[jax.ShapeDtypeStruct((n, w), dt) for w, dt in zip(out_w, out_dt)],
        compiler_params=_cparams(("parallel",)),
        name="proj",
    )(*ins)


def _lru_kernel(xy_ref, conv0_ref, h0_ref, cw_ref, cb_ref, wa_ref, ba_ref, wi_ref, bi_ref, lam_ref,
                y_ref, convn_ref, hn_ref, tail_sc, h_sc, *, t_last):
    j = pl.program_id(1)
    W = BRANCH_W

    @pl.when(j == 0)
    def _():
        tail_sc[...] = conv0_ref[0]
        h_sc[...] = h0_ref[0]

    x = xy_ref[0, :, 0:W]
    y = xy_ref[0, :, W:2 * W]
    tt = x.shape[0]
    tail = tail_sc[...]
    row8 = _row_iota((8, W))
    cw = cw_ref[...]
    xc = cb_ref[...] + cw[CONV_W - 1:CONV_W, :] * x
    for kback in range(1, CONV_W):
        xs = pltpu.roll(x, kback, 0)
        head = jnp.where(row8 < kback, pltpu.roll(tail, kback, 0), xs[0:8])
        xs = head if tt == 8 else jnp.concatenate([head, xs[8:]], axis=0)
        xc = xc + cw[CONV_W - 1 - kback:CONV_W - kback, :] * xs
    r = _sigmoid(_dot(xc, wa_ref[...]) + ba_ref[...])
    ig = _sigmoid(_dot(xc, wi_ref[...]) + bi_ref[...])
    log_a = (-LRU_C) * r * _softplus(-lam_ref[...])
    a = jnp.exp(log_a)
    u = jnp.sqrt(-jnp.tanh(log_a) * (a * a + 1.0)) * (ig * xc)
    ap, hloc = _scan_rows(a, u)
    h = ap * h_sc[...] + hloc
    y_ref[0] = h * _gelu_tanh(y)
    h_sc[...] = h[tt - 1:tt, :]
    tail_sc[...] = x[tt - 8:tt, :]

    @pl.when(j == pl.num_programs(1) - 1)
    def _():
        hn_ref[0] = h[t_last - 1:t_last, :]
        convn_ref[0] = x[t_last - 8:t_last, :] if t_last >= 8 else jnp.where(
            row8 < 8 - t_last, pltpu.roll(tail, 8 - t_last, 0), pltpu.roll(x[0:8], 8 - t_last, 0))


def _lru(xy, conv0, h0, lw, tt, t_valid):
    b, tp, _ = xy.shape
    W = BRANCH_W
    nt = tp // tt
    t_last = t_valid - (nt - 1) * tt
    full = lambda a: pl.BlockSpec(a.shape, lambda i, j: (0,) * a.ndim)
    ws = (lw['lru_conv_w'], lw['lru_conv_b'], lw['lru_wa_bd'], lw['lru_ba'], lw['lru_wi_bd'], lw['lru_bi'], lw['lru_lambda'])
    return pl.pallas_call(
        functools.partial(_lru_kernel, t_last=t_last),
        grid=(b, nt),
        in_specs=[pl.BlockSpec((1, tt, 2 * W), lambda i, j: (i, j, 0)),
                  pl.BlockSpec((1, 8, W), lambda i, j: (i, 0, 0)),
                  pl.BlockSpec((1, 1, W), lambda i, j: (i, 0, 0))] + [full(a) for a in ws],
        out_specs=[pl.BlockSpec((1, tt, W), lambda i, j: (i, j, 0)),
                   pl.BlockSpec((1, 8, W), lambda i, j: (i, 0, 0)),
                   pl.BlockSpec((1, 1, W), lambda i, j: (i, 0, 0))],
        out_shape=[jax.ShapeDtypeStruct((b, tp, W), F32),
                   jax.ShapeDtypeStruct((b, 8, W), F32),
                   jax.ShapeDtypeStruct((b, 1, W), F32)],
        scratch_shapes=[pltpu.VMEM((8, W), F32), pltpu.VMEM((1, W), F32)],
        compiler_params=_cparams(("parallel", "arbitrary")),
        name="rglru",
    )(xy, conv0, h0, *ws)


def _block_diag(w):
    n, c, d = w.shape
    eye = jnp.eye(n, dtype=w.dtype)
    return (eye[:, None, :, None] * w[:, :, None, :]).reshape(n * c, n * d)


N_PROJ = N_MAIN + D_RW_IN + LANE
N_GATE = N_HEADS * D_MODEL


def _regroup_kernel(w_ref, wp_ref, wg_ref):
    o_f = N_MAIN
    o_rw = N_MAIN + N_HEADS
    o_g = o_rw + D_RW_IN
    rows = w_ref.shape[1]
    wp_ref[0, :, 0:o_f] = w_ref[0, :, 0:o_f].astype(BF16)
    wp_ref[0, :, o_f:o_f + D_RW_IN] = w_ref[0, :, o_rw:o_g].astype(BF16)
    f_cols = jnp.concatenate([w_ref[0, :, o_f:o_rw], jnp.zeros((rows, LANE - N_HEADS), F32)], axis=1)
    wp_ref[0, :, o_f + D_RW_IN:N_PROJ] = f_cols.astype(BF16)
    wg_ref[0] = w_ref[0, :, o_g:o_g + N_GATE].astype(BF16)


def _regroup_w_in(w_in):
    depth, d, n = w_in.shape
    rows = 256
    return pl.pallas_call(
        _regroup_kernel,
        grid=(depth, d // rows),
        in_specs=[pl.BlockSpec((1, rows, n), lambda l, i: (l, i, 0))],
        out_specs=[pl.BlockSpec((1, rows, N_PROJ), lambda l, i: (l, i, 0)),
                   pl.BlockSpec((1, rows, N_GATE), lambda l, i: (l, i, 0))],
        out_shape=[jax.ShapeDtypeStruct((depth, d, N_PROJ), BF16), jax.ShapeDtypeStruct((depth, d, N_GATE), BF16)],
        compiler_params=_cparams(("parallel", "parallel")),
        name="regroup_w_in",
    )(w_in)


def _prep_layer(p, l, w_proj, w_gate):
    row = lambda a: a[l].reshape(1, -1).astype(F32)
    lw = {
        'layer': l,
        'norm1_g': row(p['norm1_g']),
        'w_proj': w_proj,
        'w_gate': w_gate,
        'fox_qg': jnp.tile(p['fox_qnorm_g'][l], N_HEADS).reshape(1, -1),
        'fox_kg': jnp.tile(p['fox_knorm_g'][l], N_HEADS).reshape(1, -1),
        'fox_fb': jnp.pad(p['fox_fbias'][l], (0, LANE - N_HEADS)).reshape(1, -1),
        'head_ones': _head_ones(),
        'lru_conv_w': p['lru_conv_w'][l],
        'lru_conv_b': row(p['lru_conv_b']),
        'lru_wa_bd': _block_diag(p['lru_wa'][l]).astype(BF16),
        'lru_ba': row(p['lru_ba']),
        'lru_wi_bd': _block_diag(p['lru_wi'][l]).astype(BF16),
        'lru_bi': row(p['lru_bi']),
        'lru_lambda': row(p['lru_lambda']),
        'rw_mu': row(p['rw_mu']),
        'rw_w0': row(p['rw_w0']),
        'rw_a0': row(p['rw_a0']),
        'rw_w2p': jnp.pad(p['rw_w2'][l], ((0, 96), (0, 0))).astype(BF16),
        'rw_a2p': jnp.pad(p['rw_a2'][l], ((32, 64), (0, 0))).astype(BF16),
        'rw_g2p': jnp.pad(p['rw_g2'][l], ((64, 0), (0, 0))).astype(BF16),
        'rw_kk': row(p['rw_kk']),
        'rw_ka': row(p['rw_ka']),
        'rw_rk': row(p['rw_rk']),
        'rw_ln_g': row(p['rw_ln_g']),
        'rw_ln_b': row(p['rw_ln_b']),
        'w_branch': p['w_branch'][l].astype(BF16),
        'w_out': p['w_out'][l].astype(BF16),
        'norm2_g': row(p['norm2_g']),
        'moe_w1': p['moe_w1'][l].astype(BF16),
        'moe_w3': p['moe_w3'][l].astype(BF16),
        'moe_w2': p['moe_w2'][l].astype(BF16),
    }
    w_r = jnp.concatenate([p['moe_we'][l], p['moe_wg'][l]], axis=1)
    lw['moe_wr'] = jnp.pad(w_r, ((0, 0), (0, LANE - w_r.shape[1])))
    b_r = jnp.concatenate([p['moe_be'][l], p['moe_bg'][l]])
    lw['moe_br'] = jnp.pad(b_r, (0, LANE - b_r.shape[0])).reshape(1, -1)
    return lw


def _rwkv_kernel(pd_ref, shift0_ref, st0_ref, mu_ref, w0_ref, w2_ref, a0_ref, a2_ref, g2_ref,
                 kkw_ref, ka_ref, rk_ref, lng_ref, lnb_ref, hs_ref,
                 y_ref, stn_ref, prev_sc, st_sc, *, t_valid):
    j = pl.program_id(1)
    W = BRANCH_W
    nb, C = pd_ref.shape[0], pd_ref.shape[1]
    R4 = N_HEADS * C

    @pl.when(j == 0)
    def _():
        prev_sc[...] = shift0_ref[...]
        st_sc[...] = st0_ref[...]

    hs = hs_ref[...]
    row = _row_iota((C, 1))
    live = (row + j * C) < t_valid
    rr = _row_iota((R4, R4))
    cc = _col_iota((R4, R4))
    same_head = (rr // C) == (cc // C)
    strict = same_head & ((cc % C) < (rr % C))
    incl = same_head & ((cc % C) <= (rr % C))
    eye = rr == cc
    tile4 = lambda x: jnp.concatenate([x] * N_HEADS, axis=0)
    stack = lambda x: jnp.concatenate([x[:, h * HEAD_DIM:(h + 1) * HEAD_DIM] for h in range(N_HEADS)], axis=0)

    def chain(bi):
        pd = pd_ref[bi]
        prv = jnp.where(row == 0, prev_sc[bi], pltpu.roll(pd, 1, 0))
        ps = pd + (prv - pd) * mu_ref[...]
        r = ps[:, 0:W]
        k = ps[:, W:2 * W]
        v = ps[:, 2 * W:3 * W]
        tail = ps[:, 3 * W:3 * W + RW_TAIL]
        w_log = -_softplus(-(w0_ref[...] + _dot(jnp.tanh(tail), w2_ref[...]))) - 0.5
        logw = -jnp.exp(w_log)
        a = _sigmoid(a0_ref[...] + _dot(tail, a2_ref[...]))
        g = _dot(_sigmoid(tail), g2_ref[...])
        kk = k * kkw_ref[...]
        kk_ss = _dot2(kk * kk, hs)
        yield
        kk = kk / jnp.maximum(jnp.sqrt(kk_ss), 1e-12)
        k = k * (1.0 + (a - 1.0) * ka_ref[...])
        logw = jnp.where(live, logw, 0.0)
        kk = jnp.where(live, kk, 0.0)
        kl = jnp.where(live, k, 0.0)

        c = _cumsum_rows(logw)
        e_pos = jnp.exp(c)
        e_neg = jnp.exp(-c)
        e_end = e_pos[C - 1:C, :]
        a_bd = jnp.where(same_head, tile4(-kk * jnp.exp(c - logw)), 0.0)
        r_bd = jnp.where(same_head, tile4(r * e_pos), 0.0)
        b_t = kk * a * e_neg
        k_t = kl * e_neg
        ar = jnp.concatenate([a_bd, r_bd], axis=0).astype(BF16)
        bk = jnp.concatenate([tile4(b_t), tile4(k_t)], axis=0).astype(BF16)
        m = _dot_nt(ar, bk)
        st = st_sc[bi]
        ars = _dot(ar, st)
        v_st = stack(v)
        yield
        L = jnp.where(strict, m[0:R4, 0:R4], 0.0)
        X = ars[0:R4] + _dot(jnp.where(strict, m[0:R4, R4:2 * R4], 0.0), v_st)
        yield
        s = 1
        while s < C:
            X = X + _dot(L, X)
            s *= 2
            if s < C:
                L = _dot(L, L)
            yield
        y_st = (ars[R4:2 * R4] + _dot(jnp.where(incl, m[R4:2 * R4, 0:R4], 0.0), X)
                + _dot(jnp.where(incl, m[R4:2 * R4, R4:2 * R4], 0.0), v_st))
        bg_bd = jnp.where(same_head, tile4(b_t * e_end), 0.0)
        kg_bd = jnp.where(same_head, tile4(k_t * e_end), 0.0)
        e_col = jnp.sum(jnp.where(eye, jnp.broadcast_to(e_end, (R4, R4)), 0.0), axis=1, keepdims=True)
        st_sc[bi] = st * e_col + _dot_tn(bg_bd, X) + _dot_tn(kg_bd, v_st)
        yield

        y = jnp.concatenate([y_st[h * C:(h + 1) * C] for h in range(N_HEADS)], axis=1)
        inv_d = 1.0 / HEAD_DIM
        mean = _dot2(y, hs) * inv_d
        dv = y - mean
        var = _dot2(dv * dv, hs) * inv_d
        yn = dv * lax.rsqrt(var + RW_GN_EPS) * lng_ref[...] + lnb_ref[...]
        bonus = _dot2(r * k * rk_ref[...], hs) * v
        y_ref[bi] = (yn + bonus) * g
        prev_sc[bi] = pd[C - 1:C, :]

    chains = [chain(bi) for bi in range(nb)]
    while chains:
        chains = [ch for ch in chains if next(ch, chains) is not chains]

    @pl.when(j == pl.num_programs(1) - 1)
    def _():
        stn_ref[...] = st_sc[...]


def _rwkv(pd, shift0, s0, lw, t_valid, nb):
    b, tp, _ = pd.shape
    W = BRANCH_W
    C = RW_CHUNK
    st0 = jnp.swapaxes(s0, 2, 3).reshape(b, W, HEAD_DIM)
    full = lambda a: pl.BlockSpec(a.shape, lambda i, j: (0,) * a.ndim)
    ws = (lw['rw_mu'], lw['rw_w0'], lw['rw_w2p'], lw['rw_a0'], lw['rw_a2p'], lw['rw_g2p'],
          lw['rw_kk'], lw['rw_ka'], lw['rw_rk'], lw['rw_ln_g'], lw['rw_ln_b'], lw['head_ones'])
    y, stn = pl.pallas_call(
        functools.partial(_rwkv_kernel, t_valid=t_valid),
        grid=(b // nb, tp // C),
        in_specs=[pl.BlockSpec((nb, C, D_RW_IN), lambda i, j: (i, j, 0)),
                  pl.BlockSpec((nb, 1, D_RW_IN), lambda i, j: (i, 0, 0)),
                  pl.BlockSpec((nb, W, HEAD_DIM), lambda i, j: (i, 0, 0))] + [full(a) for a in ws],
        out_specs=[pl.BlockSpec((nb, C, W), lambda i, j: (i, j, 0)),
                   pl.BlockSpec((nb, W, HEAD_DIM), lambda i, j: (i, 0, 0))],
        out_shape=[jax.ShapeDtypeStruct((b, tp, W), F32),
                   jax.ShapeDtypeStruct((b, W, HEAD_DIM), F32)],
        scratch_shapes=[pltpu.VMEM((nb, 1, D_RW_IN), F32), pltpu.VMEM((nb, W, HEAD_DIM), F32)],
        compiler_params=_cparams(("parallel", "arbitrary")),
        name="rwkv7",
    )(pd, shift0, st0, *ws)
    return y, jnp.swapaxes(stn.reshape(b, N_HEADS, HEAD_DIM, HEAD_DIM), 2, 3)


ATT_BLK = 256


def _suffix_ones(n):
    i = jnp.arange(n)
    return jnp.concatenate([(i[:, None] > i[None, :]), jnp.ones((n, n), bool)], axis=1).astype(BF16)


def _sb_prompt_kernel(q_ref, k_ref, v_ref, m_ref, o_ref, run_sc, acc_sc):
    i = pl.program_id(1)
    T = ATT_BLK
    msuf = m_ref[...]
    lower = _col_iota((T, T)) < _row_iota((T, T))
    heads = [slice(h * HEAD_DIM, (h + 1) * HEAD_DIM) for h in range(N_HEADS)]
    run_sc[...] = jnp.zeros_like(run_sc)
    acc_sc[...] = jnp.zeros_like(acc_sc)

    def block(kj, diag):
        start = pl.multiple_of(kj * T, T)
        zs = [_dot_nt(q_ref[0, :, sl], k_ref[0, pl.ds(start, T), sl]) for sl in heads]
        log_betas = [jnp.minimum(z, 0.0) - jnp.log(1.0 + jnp.exp(-jnp.abs(z))) for z in zs]
        lss = [lb - z for lb, z in zip(log_betas, zs)]
        if diag:
            lss = [jnp.where(lower, ls, 0.0) for ls in lss]
        sts = [_dot2(ls, msuf) for ls in lss]
        ws = [jnp.exp(lb + st + run_sc[h]) for h, (lb, st) in enumerate(zip(log_betas, sts))]
        if diag:
            ws = [jnp.where(lower, w, 0.0) for w in ws]
        for h, sl in enumerate(heads):
            acc_sc[h] += _dot(ws[h], v_ref[0, pl.ds(start, T), sl])
            run_sc[h] += sts[h][:, 0:1] + lss[h][:, 0:1]

    block(i, True)

    def body(jj, carry):
        block(i - 1 - jj, False)
        return carry

    lax.fori_loop(0, i, body, 0)
    o_ref[0] = jnp.concatenate([acc_sc[h] for h in range(N_HEADS)], axis=1)


def _sb_prompt(q, k, v):
    b, t, W = q.shape
    msuf = _suffix_ones(ATT_BLK)[:, :ATT_BLK]
    return pl.pallas_call(
        _sb_prompt_kernel,
        grid=(b, t // ATT_BLK),
        in_specs=[pl.BlockSpec((1, ATT_BLK, W), lambda i, j: (i, j, 0)),
                  pl.BlockSpec((1, t, W), lambda i, j: (i, 0, 0)),
                  pl.BlockSpec((1, t, W), lambda i, j: (i, 0, 0)),
                  pl.BlockSpec(msuf.shape, lambda i, j: (0, 0))],
        out_specs=pl.BlockSpec((1, ATT_BLK, W), lambda i, j: (i, j, 0)),
        out_shape=jax.ShapeDtypeStruct((b, t, W), F32),
        scratch_shapes=[pltpu.VMEM((N_HEADS, ATT_BLK, 1), F32), pltpu.VMEM((N_HEADS, ATT_BLK, HEAD_DIM), F32)],
        compiler_params=_cparams(("parallel", "arbitrary")),
        name="sb_prompt",
    )(q, k, v, msuf)


def _fox_prompt_kernel(q_ref, k_ref, v_ref, fq_ref, fk_ref, o_ref, m_sc, acc_sc):
    i = pl.program_id(1)
    T = ATT_BLK
    lower = _col_iota((T, T)) <= _row_iota((T, T))
    heads = [slice(h * HEAD_DIM, (h + 1) * HEAD_DIM) for h in range(N_HEADS)]
    m_sc[...] = jnp.full_like(m_sc, -1e30)
    acc_sc[...] = jnp.zeros_like(acc_sc)

    def block(kj, diag):
        start = pl.multiple_of(kj * T, T)
        ss = [_dot_nt(q_ref[0, :, sl], k_ref[0, pl.ds(start, T), sl])
              + (fq_ref[0, :, h:h + 1] - fk_ref[0, h:h + 1, pl.ds(start, T)]) for h, sl in enumerate(heads)]
        if diag:
            ss = [jnp.where(lower, s, -jnp.inf) for s in ss]
        m_olds = [m_sc[h] for h in range(N_HEADS)]
        m_news = [jnp.maximum(m, jnp.max(s, axis=1, keepdims=True)) for m, s in zip(m_olds, ss)]
        ps = [jnp.exp(s - m) for s, m in zip(ss, m_news)]
        for h in range(N_HEADS):
            alpha = jnp.exp(m_olds[h] - m_news[h])
            acc_sc[h] = acc_sc[h] * alpha + _dot(ps[h], v_ref[0, pl.ds(start, T), h * LANE:(h + 1) * LANE])
            m_sc[h] = m_news[h]

    block(i, True)

    def body(jj, carry):
        block(i - 1 - jj, False)
        return carry

    lax.fori_loop(0, i, body, 0)
    o_ref[0] = jnp.concatenate([acc_sc[h][:, 0:HEAD_DIM] / acc_sc[h][:, HEAD_DIM:LANE] for h in range(N_HEADS)],
                               axis=1)


def _fox_prompt(q, k, v, f_col, f_row):
    b, t, W = q.shape
    v4 = v.reshape(b, t, N_HEADS, HEAD_DIM)
    v_ones = jnp.concatenate([v4, jnp.ones_like(v4)], axis=3).reshape(b, t, N_HEADS * LANE)
    return pl.pallas_call(
        _fox_prompt_kernel,
        grid=(b, t // ATT_BLK),
        in_specs=[pl.BlockSpec((1, ATT_BLK, W), lambda i, j: (i, j, 0)),
                  pl.BlockSpec((1, t, W), lambda i, j: (i, 0, 0)),
                  pl.BlockSpec((1, t, N_HEADS * LANE), lambda i, j: (i, 0, 0)),
                  pl.BlockSpec((1, ATT_BLK, 8), lambda i, j: (i, j, 0)),
                  pl.BlockSpec((1, 8, t), lambda i, j: (i, 0, 0))],
        out_specs=pl.BlockSpec((1, ATT_BLK, W), lambda i, j: (i, j, 0)),
        out_shape=jax.ShapeDtypeStruct((b, t, W), F32),
        scratch_shapes=[pltpu.VMEM((N_HEADS, ATT_BLK, 1), F32), pltpu.VMEM((N_HEADS, ATT_BLK, LANE), F32)],
        compiler_params=_cparams(("parallel", "arbitrary")),
        name="fox_prompt",
    )(q, k, v_ones, f_col, f_row)


def _cumsum_kernel(x_ref, o_ref):
    n = x_ref.shape[2] // LANE

    def body(c, carry):
        start = pl.multiple_of(c * LANE, LANE)
        s = _cumsum_lanes(x_ref[0, :, pl.ds(start, LANE)]) + carry
        o_ref[0, :, pl.ds(start, LANE)] = s
        return jnp.broadcast_to(s[:, LANE - 1:LANE], s.shape)

    lax.fori_loop(0, n, body, jnp.zeros((8, LANE), F32))


def _cumsum_prompt(logf_row):
    b, _, t = logf_row.shape
    return pl.pallas_call(
        _cumsum_kernel,
        grid=(b,),
        in_specs=[pl.BlockSpec((1, 8, t), lambda i: (i, 0, 0))],
        out_specs=pl.BlockSpec((1, 8, t), lambda i: (i, 0, 0)),
        out_shape=jax.ShapeDtypeStruct((b, 8, t), F32),
        compiler_params=_cparams(("parallel",)),
        name="logf_cumsum",
    )(logf_row)


PAGES_PER_STEP = 8


QROWS = N_HEADS * 8


def _sb_sample_kernel(pt_ref, q_ref, kn_ref, vn_ref, m_ref, *rest, n_new):
    G = PAGES_PER_STEP
    k_refs, v_refs = rest[0:G], rest[G:2 * G]
    o_ref, run_sc, acc_sc = rest[2 * G:]
    j = pl.program_id(1)
    q = q_ref[0]
    msuf = m_ref[...]

    def blocks(kts, vts, mask):
        zs = [jnp.dot(q, kt.astype(BF16), preferred_element_type=F32) for kt in kts]
        ls = [-_softplus(z) for z in zs]
        if mask is not None:
            ls = [jnp.where(mask, l, 0.0) for l in ls]
        st = _dot2(jnp.concatenate(ls, axis=0) if len(ls) > 1 else ls[0], msuf)
        run = run_sc[...]
        acc = acc_sc[...]
        for g in range(len(kts)):
            sg = st[g * QROWS:(g + 1) * QROWS]
            w = jnp.exp(zs[g] + ls[g] + sg[:, 0:LANE] + run)
            if mask is not None:
                w = jnp.where(mask, w, 0.0)
            acc = acc + _dot_nt(w, vts[g])
            run = run + sg[:, LANE:2 * LANE]
        run_sc[...] = run
        acc_sc[...] = acc

    @pl.when(j == 0)
    def _():
        run_sc[...] = jnp.zeros_like(run_sc)
        acc_sc[...] = jnp.zeros_like(acc_sc)
        t = _row_iota((QROWS, LANE)) % 8
        s = _col_iota((QROWS, LANE))
        blocks([kn_ref[0]], [vn_ref[0]], (s < t) & (s < n_new))

    blocks([k_refs[g][0, 0] for g in range(G)], [v_refs[g][0, 0] for g in range(G)], None)

    @pl.when(j == pl.num_programs(1) - 1)
    def _():
        o_ref[0] = acc_sc[...]


def _sb_sample(q, kn_t, vn_t, cache_kt, cache_vt, page_table, layer, n_new):
    b = q.shape[0]
    n_pages = page_table.shape[1]
    G = PAGES_PER_STEP
    msuf = _suffix_ones(LANE)

    def page_spec(g):
        return pl.BlockSpec((1, 1, BRANCH_W, PAGE), lambda i, j, pt: (layer, pt[i, n_pages - 1 - (j * G + g)], 0, 0))

    new_spec = pl.BlockSpec((1, BRANCH_W, LANE), lambda i, j, pt: (i, 0, 0))
    out_spec = pl.BlockSpec((1, QROWS, BRANCH_W), lambda i, j, pt: (i, 0, 0))
    return pl.pallas_call(
        functools.partial(_sb_sample_kernel, n_new=n_new),
        grid_spec=pltpu.PrefetchScalarGridSpec(
            num_scalar_prefetch=1,
            grid=(b, n_pages // G),
            in_specs=[out_spec, new_spec, new_spec, pl.BlockSpec(msuf.shape, lambda i, j, pt: (0, 0))]
                     + [page_spec(g) for g in range(G)] * 2,
            out_specs=out_spec,
            scratch_shapes=[pltpu.VMEM((QROWS, LANE), F32), pltpu.VMEM((QROWS, BRANCH_W), F32)]),
        out_shape=jax.ShapeDtypeStruct((b, QROWS, BRANCH_W), F32),
        compiler_params=_cparams(("parallel", "arbitrary")),
        name="sb_sample",
    )(page_table, q, kn_t, vn_t, msuf, *([cache_kt] * G), *([cache_vt] * G))


def _fox_sample_kernel(pt_ref, q_ref, fq_ref, kn_ref, vn_ref, f_ref, fn_ref, *rest, n_new):
    G = PAGES_PER_STEP
    k_refs, v_refs = rest[0:G], rest[G:2 * G]
    o_ref, m_sc, l_sc, acc_sc = rest[2 * G:]
    j = pl.program_id(1)
    q = q_ref[0]
    fq = fq_ref[0]

    def head_rows(f):
        return jnp.concatenate([jnp.broadcast_to(f[h:h + 1, :], (8, LANE)) for h in range(N_HEADS)], axis=0)

    def blocks(kts, vts, fks, mask):
        ss = [jnp.dot(q, kt.astype(BF16), preferred_element_type=F32) + (fq - head_rows(fk))
              for kt, fk in zip(kts, fks)]
        if mask is not None:
            ss = [jnp.where(mask, s, -jnp.inf) for s in ss]
        s_max = ss[0]
        for s in ss[1:]:
            s_max = jnp.maximum(s_max, s)
        m_old = m_sc[...]
        m_new = jnp.maximum(m_old, jnp.max(s_max, axis=1, keepdims=True))
        alpha = jnp.exp(m_old - m_new)
        ps = [jnp.exp(s - m_new) for s in ss]
        p_sum = ps[0]
        pv = _dot_nt(ps[0], vts[0])
        for p, vt in zip(ps[1:], vts[1:]):
            p_sum = p_sum + p
            pv = pv + _dot_nt(p, vt)
        l_sc[...] = l_sc[...] * alpha + jnp.sum(p_sum, axis=1, keepdims=True)
        acc_sc[...] = acc_sc[...] * alpha + pv
        m_sc[...] = m_new

    @pl.when(j == 0)
    def _():
        m_sc[...] = jnp.full_like(m_sc, -1e30)
        l_sc[...] = jnp.zeros_like(l_sc)
        acc_sc[...] = jnp.zeros_like(acc_sc)
        t = _row_iota((QROWS, LANE)) % 8
        s = _col_iota((QROWS, LANE))
        blocks([kn_ref[0]], [vn_ref[0]], [fn_ref[0, 0]], (s <= t) & (s < n_new))

    blocks([k_refs[g][0, 0] for g in range(G)], [v_refs[g][0, 0] for g in range(G)],
           [f_ref[0, g] for g in range(G)], None)

    @pl.when(j == pl.num_programs(1) - 1)
    def _():
        o_ref[0] = acc_sc[...] / l_sc[...]


def _fox_sample(q, fq, kn_t, vn_t, f_all, cache_kt, cache_vt, page_table, layer, n_new):
    b = q.shape[0]
    n_pages = page_table.shape[1]
    G = PAGES_PER_STEP

    def page_spec(g):
        return pl.BlockSpec((1, 1, BRANCH_W, PAGE), lambda i, j, pt: (layer, pt[i, j * G + g], 0, 0))

    new_spec = pl.BlockSpec((1, BRANCH_W, LANE), lambda i, j, pt: (i, 0, 0))
    out_spec = pl.BlockSpec((1, QROWS, BRANCH_W), lambda i, j, pt: (i, 0, 0))
    return pl.pallas_call(
        functools.partial(_fox_sample_kernel, n_new=n_new),
        grid_spec=pltpu.PrefetchScalarGridSpec(
            num_scalar_prefetch=1,
            grid=(b, n_pages // G),
            in_specs=[out_spec,
                      pl.BlockSpec((1, QROWS, LANE), lambda i, j, pt: (i, 0, 0)),
                      new_spec, new_spec,
                      pl.BlockSpec((1, G, 8, LANE), lambda i, j, pt: (i, j, 0, 0)),
                      pl.BlockSpec((1, 1, 8, LANE), lambda i, j, pt: (i, n_pages, 0, 0))]
                     + [page_spec(g) for g in range(G)] * 2,
            out_specs=out_spec,
            scratch_shapes=[pltpu.VMEM((QROWS, 1), F32), pltpu.VMEM((QROWS, 1), F32),
                            pltpu.VMEM((QROWS, BRANCH_W), F32)]),
        out_shape=jax.ShapeDtypeStruct((b, QROWS, BRANCH_W), F32),
        compiler_params=_cparams(("parallel", "arbitrary")),
        name="fox_sample",
    )(page_table, q, fq, kn_t, vn_t, f_all, f_all, *([cache_kt] * G), *([cache_vt] * G))


def _f_sample_kernel(pt_ref, pool_ref, new_ref, o_ref, x_sc):
    i = pl.program_id(0)
    n_pages = pt_ref.shape[1]
    n = n_pages + 1

    def gather(j, carry):
        p = pt_ref[i, j]
        tile = pool_ref[0, p // 2]
        x_sc[j] = jnp.where(p % 2 == 1, pltpu.roll(tile, N_HEADS, 0), tile)
        return carry

    lax.fori_loop(0, n_pages, gather, 0)
    x_sc[n_pages] = new_ref[0]
    c = _cumsum_lanes(x_sc[...])
    tot = jnp.broadcast_to(c[:, :, LANE - 1:LANE], c.shape)
    inc = tot
    s = 1
    while s < n:
        inc = inc + jnp.concatenate([jnp.zeros((s, 8, LANE), F32), inc[0:n - s]], axis=0)
        s *= 2
    o_ref[0] = c + (inc - tot)


def _f_sample(pool_pairs, logf_new, page_table, layer):
    b, n_pages = page_table.shape
    return pl.pallas_call(
        _f_sample_kernel,
        grid_spec=pltpu.PrefetchScalarGridSpec(
            num_scalar_prefetch=1,
            grid=(b,),
            in_specs=[pl.BlockSpec((1,) + pool_pairs.shape[1:], lambda i, pt: (layer, 0, 0, 0)),
                      pl.BlockSpec((1, 8, LANE), lambda i, pt: (i, 0, 0))],
            out_specs=pl.BlockSpec((1, n_pages + 1, 8, LANE), lambda i, pt: (i, 0, 0, 0)),
            scratch_shapes=[pltpu.VMEM((n_pages + 1, 8, LANE), F32)]),
        out_shape=jax.ShapeDtypeStruct((b, n_pages + 1, 8, LANE), F32),
        compiler_params=_cparams(("arbitrary",)),
        name="logf_cumsum_paged",
    )(page_table, pool_pairs, logf_new)


def _pages_t(cache):
    l, n, p, h, d = cache.shape
    return jnp.transpose(cache, (0, 1, 3, 4, 2)).reshape(l, n, h * d, p)


def _logf_pairs(cache_logf):
    l, n, p, h = cache_logf.shape
    return jnp.swapaxes(cache_logf, 2, 3).reshape(l, n // 2, 2 * h, p)


def _rows_t(x):
    return jnp.pad(jnp.swapaxes(x, 1, 2), ((0, 0), (0, 0), (0, LANE - x.shape[1])))


def _head_mask():
    return (jnp.arange(QROWS)[:, None] // 8) == (jnp.arange(BRANCH_W)[None, :] // HEAD_DIM)


def _heads_q(q):
    b, t, _ = q.shape
    q8 = jnp.pad(q, ((0, 0), (0, 8 - t), (0, 0)))
    return jnp.where(_head_mask(), jnp.tile(q8, (1, N_HEADS, 1)), 0).astype(BF16)


def _heads_out(o, t):
    b = o.shape[0]
    o4 = jnp.where(_head_mask(), o, 0.0).reshape(b, N_HEADS, 8, BRANCH_W)
    return jnp.sum(o4, axis=1)[:, :t, :]


def _sample_sb(q, k_new, v_new, cache_kt, cache_vt, page_table, layer):
    t = q.shape[1]
    o = _sb_sample(_heads_q(q), _rows_t(k_new), _rows_t(v_new), cache_kt, cache_vt, page_table, layer, t)
    return _heads_out(o, t)


def _sample_fox(q, k_new, v_new, logf_new, cache_kt, cache_vt, pool_pairs, page_table, layer):
    b, t, _ = q.shape
    n_pages = page_table.shape[1]
    lf = jnp.pad(jnp.swapaxes(logf_new, 1, 2), ((0, 0), (0, 8 - N_HEADS), (0, LANE - t)))
    f_all = _f_sample(pool_pairs, lf, page_table, layer)
    f_q = f_all[:, n_pages, :N_HEADS, :8]
    f_q = jnp.broadcast_to(f_q.reshape(b, QROWS, 1), (b, QROWS, LANE))
    o = _fox_sample(_heads_q(q), f_q, _rows_t(k_new), _rows_t(v_new), f_all, cache_kt, cache_vt,
                    page_table, layer, t)
    return _heads_out(o, t), f_all


def _merge_kernel(x_ref, ya_ref, yb_ref, yc_ref, yd_ref, g_ref, wg_ref, wb_ref, wo_ref, o_ref):
    x = x_ref[...]
    xn = _rmsnorm(x, g_ref[...]).astype(BF16)
    m = None
    for n, y_ref in enumerate((ya_ref, yb_ref, yc_ref, yd_ref)):
        gate = _sigmoid(jnp.dot(xn, wg_ref[:, n * D_MODEL:(n + 1) * D_MODEL], preferred_element_type=F32))
        term = gate * _dot(y_ref[...], wb_ref[n])
        m = term if m is None else m + term
    o_ref[...] = x + _dot(m, wo_ref[...])


def _merge(x2d, ys, lw, tm):
    n = x2d.shape[0]
    row = lambda w: pl.BlockSpec((tm, w), lambda i: (i, 0))
    full = lambda a: pl.BlockSpec(a.shape, lambda i: (0,) * a.ndim)
    ws = (lw['norm1_g'], lw['w_gate'], lw['w_branch'], lw['w_out'])
    layer = lw['layer']
    wg_spec = pl.BlockSpec((None, D_MODEL, N_GATE), lambda i: (layer, 0, 0))
    return pl.pallas_call(
        _merge_kernel,
        grid=(n // tm,),
        in_specs=[row(D_MODEL)] + [row(BRANCH_W)] * 4 + [full(ws[0]), wg_spec, full(ws[2]), full(ws[3])],
        out_specs=row(D_MODEL),
        out_shape=jax.ShapeDtypeStruct((n, D_MODEL), F32),
        compiler_params=_cparams(("parallel",)),
        name="merge",
    )(x2d, *ys, *ws)


def _router_gate(logits):
    lane = _col_iota(logits.shape)
    lane_f = lane.astype(F32)
    big = float(LANE)
    neg = -jnp.inf
    gl = jnp.where((lane >= N_EXPERTS) & (lane < N_EXPERTS + N_GROUPS), logits, neg)
    g_max = jnp.max(gl, axis=1, keepdims=True)
    g_idx = jnp.min(jnp.where(gl == g_max, lane_f, big), axis=1, keepdims=True) - float(N_EXPERTS)
    g_w = 1.0 / jnp.sum(jnp.exp(gl - g_max), axis=1, keepdims=True)
    per_group = N_EXPERTS // N_GROUPS
    lo = g_idx * float(per_group)
    el = jnp.where((lane_f >= lo) & (lane_f < lo + float(per_group)), logits, neg)
    m1 = jnp.max(el, axis=1, keepdims=True)
    i1 = jnp.min(jnp.where(el == m1, lane_f, big), axis=1, keepdims=True)
    el2 = jnp.where(lane_f == i1, neg, el)
    m2 = jnp.max(el2, axis=1, keepdims=True)
    i2 = jnp.min(jnp.where(el2 == m2, lane_f, big), axis=1, keepdims=True)
    d = jnp.exp(m2 - m1)
    w1 = g_w / (1.0 + d)
    return jnp.where(lane_f == i1, w1, 0.0) + jnp.where(lane_f == i2, w1 * d, 0.0)


def _moe_kernel(x_ref, g_ref, wr_ref, br_ref, w1_ref, w3_ref, w2_ref, o_ref, xn_sc, gate_sc, acc_sc):
    e = pl.program_id(1)

    @pl.when(e == 0)
    def _():
        x = x_ref[...]
        xn = _rmsnorm(x, g_ref[...])
        xn_sc[...] = xn.astype(BF16)
        logits = jnp.dot(xn, wr_ref[...], preferred_element_type=F32, precision=lax.Precision.HIGHEST) + br_ref[...]
        gate_sc[...] = _router_gate(logits)
        acc_sc[...] = x

    xn = xn_sc[...]
    a = jnp.dot(xn, w1_ref[0], preferred_element_type=F32)
    h = a * _sigmoid(a) * jnp.dot(xn, w3_ref[0], preferred_element_type=F32)
    ge = jnp.sum(jnp.where(_col_iota(gate_sc.shape) == e, gate_sc[...], 0.0), axis=1, keepdims=True)
    acc_sc[...] += _dot(h * ge, w2_ref[0])

    @pl.when(e == pl.num_programs(1) - 1)
    def _():
        o_ref[...] = acc_sc[...]


def _moe(x2d, lw, tm):
    n = x2d.shape[0]
    row = pl.BlockSpec((tm, D_MODEL), lambda i, e: (i, 0))
    full = lambda a: pl.BlockSpec(a.shape, lambda i, e: (0,) * a.ndim)
    return pl.pallas_call(
        _moe_kernel,
        grid=(n // tm, N_EXPERTS),
        in_specs=[row, full(lw['norm2_g']), full(lw['moe_wr']), full(lw['moe_br']),
                  pl.BlockSpec((1, D_MODEL, D_EXP), lambda i, e: (e, 0, 0)),
                  pl.BlockSpec((1, D_MODEL, D_EXP), lambda i, e: (e, 0, 0)),
                  pl.BlockSpec((1, D_EXP, D_MODEL), lambda i, e: (e, 0, 0))],
        out_specs=row,
        out_shape=jax.ShapeDtypeStruct((n, D_MODEL), F32),
        scratch_shapes=[pltpu.VMEM((tm, D_MODEL), BF16), pltpu.VMEM((tm, LANE), F32), pltpu.VMEM((tm, D_MODEL), F32)],
        compiler_params=_cparams(("parallel", "arbitrary")),
        name="moe",
    )(x2d, lw['norm2_g'], lw['moe_wr'], lw['moe_br'], lw['moe_w1'], lw['moe_w3'], lw['moe_w2'])


def _pad_rows(a, mult):
    t = a.shape[1]
    tp = -(-t // mult) * mult
    return a if tp == t else jnp.pad(a, ((0, 0), (0, tp - t), (0, 0)))


def _layer(x, lw, state, paged):
    b, t, _ = x.shape
    n = b * t
    W = BRANCH_W
    conv0, h0, shift0, wkv0 = state
    tm = min(n, 256)
    x2d = x.reshape(n, D_MODEL)
    (lru_xy, rw_in, sbq, sbk, sbv, sbkb, sbvb, fq, fk, fv, fkb, fvb, logf_pad) = _proj(x2d, lw, tm)
    r3 = lambda a: a.reshape(b, t, a.shape[-1])
    logf = r3(logf_pad[:, :N_HEADS])

    tt = 256 if t % 256 == 0 else 8
    y_a, conv_n, h_n = _lru(_pad_rows(r3(lru_xy), tt), jnp.pad(conv0, ((0, 0), (8 - (CONV_W - 1), 0), (0, 0))),
                            h0[:, None, :], lw, tt, t)
    rw3 = r3(rw_in)
    y_d, wkv_n = _rwkv(_pad_rows(rw3, RW_CHUNK), shift0[:, None, :], wkv0, lw, t, 4)

    if paged is None:
        y_b = _sb_prompt(r3(sbq), r3(sbkb), r3(sbvb))
        f_row = _cumsum_prompt(jnp.pad(jnp.swapaxes(logf, 1, 2), ((0, 0), (0, 8 - N_HEADS), (0, 0))))
        y_c = _fox_prompt(r3(fq), r3(fkb), r3(fvb), jnp.swapaxes(f_row, 1, 2), f_row)
    else:
        sb_kt, sb_vt, fox_kt, fox_vt, logf_pairs, page_table, layer = paged
        y_b = _sample_sb(r3(sbq), r3(sbk), r3(sbv), sb_kt, sb_vt, page_table, layer)
        y_c, _ = _sample_fox(r3(fq), r3(fk), r3(fv), logf, fox_kt, fox_vt, logf_pairs, page_table, layer)

    ys = (y_a[:, :t].reshape(n, W), y_b.reshape(n, W), y_c.reshape(n, W), y_d[:, :t].reshape(n, W))
    x1 = _merge(x2d, ys, lw, tm)
    x2 = _moe(x1, lw, min(n, 512))
    hd = lambda a: a.reshape(b, t, N_HEADS, HEAD_DIM)
    new_state = (hd(sbk), hd(sbv), hd(fk), hd(fv), logf, conv_n[:, 8 - (CONV_W - 1):], h_n[:, 0], rw3[:, t - 1], wkv_n)
    return x2.reshape(b, t, D_MODEL), new_state


def kernel(x_prompt, x_sample, cache_sb_k, cache_sb_v, cache_fox_k, cache_fox_v, cache_fox_logf, state_lru_conv, state_lru_h, state_rwkv_shift, state_rwkv_wkv, page_table, norm1_g, w_in, lru_conv_w, lru_conv_b, lru_wa, lru_ba, lru_wi, lru_bi, lru_lambda, fox_qnorm_g, fox_knorm_g, fox_fbias, rw_mu, rw_w0, rw_w2, rw_a0, rw_a2, rw_g2, rw_kk, rw_ka, rw_rk, rw_ln_g, rw_ln_b, w_branch, w_out, norm2_g, moe_wg, moe_bg, moe_we, moe_be, moe_w1, moe_w3, moe_w2):
    params = dict(norm1_g=norm1_g, w_in=w_in, lru_conv_w=lru_conv_w, lru_conv_b=lru_conv_b, lru_wa=lru_wa,
                  lru_ba=lru_ba, lru_wi=lru_wi, lru_bi=lru_bi, lru_lambda=lru_lambda, fox_qnorm_g=fox_qnorm_g,
                  fox_knorm_g=fox_knorm_g, fox_fbias=fox_fbias, rw_mu=rw_mu, rw_w0=rw_w0, rw_w2=rw_w2, rw_a0=rw_a0,
                  rw_a2=rw_a2, rw_g2=rw_g2, rw_kk=rw_kk, rw_ka=rw_ka, rw_rk=rw_rk, rw_ln_g=rw_ln_g, rw_ln_b=rw_ln_b,
                  w_branch=w_branch, w_out=w_out, norm2_g=norm2_g, moe_wg=moe_wg, moe_bg=moe_bg, moe_we=moe_we,
                  moe_be=moe_be, moe_w1=moe_w1, moe_w3=moe_w3, moe_w2=moe_w2)
    depth = w_in.shape[0]
    bp = x_prompt.shape[0]
    sb_kt, sb_vt = _pages_t(cache_sb_k), _pages_t(cache_sb_v)
    fox_kt, fox_vt = _pages_t(cache_fox_k), _pages_t(cache_fox_v)
    logf_pairs = _logf_pairs(cache_fox_logf)
    prompt_init = (jnp.zeros((bp, CONV_W - 1, BRANCH_W), F32), jnp.zeros((bp, BRANCH_W), F32),
                   jnp.zeros((bp, D_RW_IN), F32), jnp.zeros((bp, N_HEADS, HEAD_DIM, HEAD_DIM), F32))
    w_proj, w_gate = _regroup_w_in(w_in)
    y_p, y_s = x_prompt, x_sample
    st_p, st_s = [], []
    for l in range(depth):
        lw = _prep_layer(params, l, w_proj, w_gate)
        y_p, sp = _layer(y_p, lw, prompt_init, None)
        y_s, ss = _layer(y_s, lw, (state_lru_conv[l], state_lru_h[l], state_rwkv_shift[l], state_rwkv_wkv[l]),
                         (sb_kt, sb_vt, fox_kt, fox_vt, logf_pairs, page_table, l))
        st_p.append(sp)
        st_s.append(ss)
    stack = lambda sts: [jnp.stack([s[i] for s in sts], axis=0) for i in range(9)]
    return (y_p, y_s, *stack(st_p), *stack(st_s))
```

```python
import functools
import math

import jax
import jax.numpy as jnp
from jax import lax
from jax.experimental import pallas as pl
from jax.experimental.pallas import tpu as pltpu

F32 = jnp.float32
BF16 = jnp.bfloat16

D_MODEL = 1024
HEAD_DIM = 64
N_HEADS = 4
BRANCH_W = N_HEADS * HEAD_DIM
CONV_W = 4
LRU_C = 8.0
PAGE = 128
RW_TAIL = 128
D_RW_IN = 3 * BRANCH_W + RW_TAIL
N_MAIN = 8 * BRANCH_W
N_GROUPS = 4
N_EXPERTS = 16
D_EXP = 256
RMS_EPS = 1e-6
RW_GN_EPS = 64e-5
ATT_SCALE = HEAD_DIM ** -0.5
LANE = 128
RW_CHUNK = 64
VMEM_LIMIT = 56 * 1024 * 1024


def _cparams(sem):
    return pltpu.CompilerParams(dimension_semantics=sem, vmem_limit_bytes=VMEM_LIMIT)


def _dot(a, b):
    return jnp.dot(a.astype(BF16), b.astype(BF16), preferred_element_type=F32)


def _dot_nt(a, b):
    return lax.dot_general(a.astype(BF16), b.astype(BF16), (((1,), (1,)), ((), ())), preferred_element_type=F32)


def _dot_tn(a, b):
    return lax.dot_general(a.astype(BF16), b.astype(BF16), (((0,), (0,)), ((), ())), preferred_element_type=F32)


def _dot2(a, b_bf16):
    hi = a.astype(BF16)
    lo = (a - hi.astype(F32)).astype(BF16)
    return (jnp.dot(hi, b_bf16, preferred_element_type=F32) + jnp.dot(lo, b_bf16, preferred_element_type=F32))


def _softplus(x):
    return jnp.maximum(x, 0.0) + jnp.log1p(jnp.exp(-jnp.abs(x)))


def _sigmoid(x):
    return 1.0 / (1.0 + jnp.exp(-x))


def _gelu_tanh(x):
    return 0.5 * x * (1.0 + jnp.tanh(math.sqrt(2.0 / math.pi) * (x + 0.044715 * (x * x * x))))


def _rmsnorm(x, g):
    return x * lax.rsqrt(jnp.mean(x * x, axis=-1, keepdims=True) + RMS_EPS) * g


def _row_iota(shape):
    return lax.broadcasted_iota(jnp.int32, shape, 0)


def _col_iota(shape):
    return lax.broadcasted_iota(jnp.int32, shape, 1)


def _scan_rows(a, u):
    n = a.shape[0]
    row = _row_iota(a.shape)
    s = 1
    while s < n:
        valid = row >= s
        a_sh = pltpu.roll(a, s, 0)
        u_sh = pltpu.roll(u, s, 0)
        u = jnp.where(valid, a * u_sh + u, u)
        a = jnp.where(valid, a * a_sh, a)
        s *= 2
    return a, u


def _cumsum_rows(x):
    n = x.shape[0]
    row = _row_iota(x.shape)
    s = 1
    while s < n:
        x = x + jnp.where(row >= s, pltpu.roll(x, s, 0), 0.0)
        s *= 2
    return x


def _cumsum_lanes(x):
    ax = x.ndim - 1
    n = x.shape[ax]
    col = lax.broadcasted_iota(jnp.int32, x.shape, ax)
    s = 1
    while s < n:
        x = x + jnp.where(col >= s, pltpu.roll(x, s, ax), 0.0)
        s *= 2
    return x


def _head_ones():
    i = jnp.arange(BRANCH_W) // HEAD_DIM
    return (i[:, None] == i[None, :]).astype(BF16)


def _proj_kernel(x_ref, g_ref, w_ref, qg_ref, kg_ref, fb_ref, hs_ref,
                 lru_ref, rw_ref, sbq_ref, sbk_ref, sbv_ref, sbkb_ref, sbvb_ref,
                 fq_ref, fk_ref, fv_ref, fkb_ref, fvb_ref, logf_ref):
    xn = _rmsnorm(x_ref[...], g_ref[...]).astype(BF16)

    def seg(a, b):
        return jnp.dot(xn, w_ref[:, a:b], preferred_element_type=F32)

    W = BRANCH_W
    lru_ref[...] = seg(0, 2 * W)
    sbq_ref[...] = (seg(2 * W, 3 * W) * ATT_SCALE).astype(BF16)
    k = seg(3 * W, 4 * W)
    sbk_ref[...] = k
    sbkb_ref[...] = k.astype(BF16)
    v = seg(4 * W, 5 * W)
    sbv_ref[...] = v
    sbvb_ref[...] = v.astype(BF16)
    hs = hs_ref[...]
    q = seg(5 * W, 6 * W)
    q = q * lax.rsqrt(_dot2(q * q, hs) * (1.0 / HEAD_DIM) + RMS_EPS) * qg_ref[...]
    fq_ref[...] = (q * ATT_SCALE).astype(BF16)
    k = seg(6 * W, 7 * W)
    k = k * lax.rsqrt(_dot2(k * k, hs) * (1.0 / HEAD_DIM) + RMS_EPS) * kg_ref[...]
    fk_ref[...] = k
    fkb_ref[...] = k.astype(BF16)
    v = seg(7 * W, 8 * W)
    fv_ref[...] = v
    fvb_ref[...] = v.astype(BF16)
    rw_ref[...] = seg(N_MAIN, N_MAIN + D_RW_IN)
    f = seg(N_MAIN + D_RW_IN, N_MAIN + D_RW_IN + LANE) + fb_ref[...]
    logf_ref[...] = -_softplus(-f)


def _proj(x2d, lw, tm):
    n = x2d.shape[0]
    W = BRANCH_W
    row = lambda w: pl.BlockSpec((tm, w), lambda i: (i, 0))
    full = lambda a: pl.BlockSpec(a.shape, lambda i: (0,) * a.ndim)
    ins = (x2d, lw['norm1_g'], lw['w_proj'], lw['fox_qg'], lw['fox_kg'], lw['fox_fb'], lw['head_ones'])
    out_w = (2 * W, D_RW_IN, W, W, W, W, W, W, W, W, W, W, LANE)
    out_dt = (F32, F32, BF16, F32, F32, BF16, BF16, BF16, F32, F32, BF16, BF16, F32)
    layer = lw['layer']
    w_spec = pl.BlockSpec((None, D_MODEL, N_PROJ), lambda i: (layer, 0, 0))
    return pl.pallas_call(
        _proj_kernel,
        grid=(n // tm,),
        in_specs=[row(D_MODEL), full(ins[1]), w_spec] + [full(a) for a in ins[3:]],
        out_specs=[row(w) for w in out_w],
        out_shape=[jax.ShapeDtypeStruct((n, w), dt) for w, dt in zip(out_w, out_dt)],
        compiler_params=_cparams(("parallel",)),
        name="proj",
    )(*ins)


def _lru_kernel(xy_ref, conv0_ref, h0_ref, cw_ref, cb_ref, wa_ref, ba_ref, wi_ref, bi_ref, lam_ref,
                y_ref, convn_ref, hn_ref, tail_sc, h_sc, *, t_last):
    j = pl.program_id(1)
    W = BRANCH_W

    @pl.when(j == 0)
    def _():
        tail_sc[...] = conv0_ref[0]
        h_sc[...] = h0_ref[0]

    x = xy_ref[0, :, 0:W]
    y = xy_ref[0, :, W:2 * W]
    tt = x.shape[0]
    tail = tail_sc[...]
    row8 = _row_iota((8, W))
    cw = cw_ref[...]
    xc = cb_ref[...] + cw[CONV_W - 1:CONV_W, :] * x
    for kback in range(1, CONV_W):
        xs = pltpu.roll(x, kback, 0)
        head = jnp.where(row8 < kback, pltpu.roll(tail, kback, 0), xs[0:8])
        xs = head if tt == 8 else jnp.concatenate([head, xs[8:]], axis=0)
        xc = xc + cw[CONV_W - 1 - kback:CONV_W - kback, :] * xs
    r = _sigmoid(_dot(xc, wa_ref[...]) + ba_ref[...])
    ig = _sigmoid(_dot(xc, wi_ref[...]) + bi_ref[...])
    log_a = (-LRU_C) * r * _softplus(-lam_ref[...])
    a = jnp.exp(log_a)
    u = jnp.sqrt(-jnp.tanh(log_a) * (a * a + 1.0)) * (ig * xc)
    ap, hloc = _scan_rows(a, u)
    h = ap * h_sc[...] + hloc
    y_ref[0] = h * _gelu_tanh(y)
    h_sc[...] = h[tt - 1:tt, :]
    tail_sc[...] = x[tt - 8:tt, :]

    @pl.when(j == pl.num_programs(1) - 1)
    def _():
        hn_ref[0] = h[t_last - 1:t_last, :]
        convn_ref[0] = x[t_last - 8:t_last, :] if t_last >= 8 else jnp.where(
            row8 < 8 - t_last, pltpu.roll(tail, 8 - t_last, 0), pltpu.roll(x[0:8], 8 - t_last, 0))


def _lru(xy, conv0, h0, lw, tt, t_valid):
    b, tp, _ = xy.shape
    W = BRANCH_W
    nt = tp // tt
    t_last = t_valid - (nt - 1) * tt
    full = lambda a: pl.BlockSpec(a.shape, lambda i, j: (0,) * a.ndim)
    ws = (lw['lru_conv_w'], lw['lru_conv_b'], lw['lru_wa_bd'], lw['lru_ba'], lw['lru_wi_bd'], lw['lru_bi'], lw['lru_lambda'])
    return pl.pallas_call(
        functools.partial(_lru_kernel, t_last=t_last),
        grid=(b, nt),
        in_specs=[pl.BlockSpec((1, tt, 2 * W), lambda i, j: (i, j, 0)),
                  pl.BlockSpec((1, 8, W), lambda i, j: (i, 0, 0)),
                  pl.BlockSpec((1, 1, W), lambda i, j: (i, 0, 0))] + [full(a) for a in ws],
        out_specs=[pl.BlockSpec((1, tt, W), lambda i, j: (i, j, 0)),
                   pl.BlockSpec((1, 8, W), lambda i, j: (i, 0, 0)),
                   pl.BlockSpec((1, 1, W), lambda i, j: (i, 0, 0))],
        out_shape=[jax.ShapeDtypeStruct((b, tp, W), F32),
                   jax.ShapeDtypeStruct((b, 8, W), F32),
                   jax.ShapeDtypeStruct((b, 1, W), F32)],
        scratch_shapes=[pltpu.VMEM((8, W), F32), pltpu.VMEM((1, W), F32)],
        compiler_params=_cparams(("parallel", "arbitrary")),
        name="rglru",
    )(xy, conv0, h0, *ws)


def _block_diag(w):
    n, c, d = w.shape
    eye = jnp.eye(n, dtype=w.dtype)
    return (eye[:, None, :, None] * w[:, :, None, :]).reshape(n * c, n * d)


N_PROJ = N_MAIN + D_RW_IN + LANE
N_GATE = N_HEADS * D_MODEL


def _regroup_kernel(w_ref, wp_ref, wg_ref):
    o_f = N_MAIN
    o_rw = N_MAIN + N_HEADS
    o_g = o_rw + D_RW_IN
    rows = w_ref.shape[1]
    wp_ref[0, :, 0:o_f] = w_ref[0, :, 0:o_f].astype(BF16)
    wp_ref[0, :, o_f:o_f + D_RW_IN] = w_ref[0, :, o_rw:o_g].astype(BF16)
    f_cols = jnp.concatenate([w_ref[0, :, o_f:o_rw], jnp.zeros((rows, LANE - N_HEADS), F32)], axis=1)
    wp_ref[0, :, o_f + D_RW_IN:N_PROJ] = f_cols.astype(BF16)
    wg_ref[0] = w_ref[0, :, o_g:o_g + N_GATE].astype(BF16)


def _regroup_w_in(w_in):
    depth, d, n = w_in.shape
    rows = 256
    return pl.pallas_call(
        _regroup_kernel,
        grid=(depth, d // rows),
        in_specs=[pl.BlockSpec((1, rows, n), lambda l, i: (l, i, 0))],
        out_specs=[pl.BlockSpec((1, rows, N_PROJ), lambda l, i: (l, i, 0)),
                   pl.BlockSpec((1, rows, N_GATE), lambda l, i: (l, i, 0))],
        out_shape=[jax.ShapeDtypeStruct((depth, d, N_PROJ), BF16), jax.ShapeDtypeStruct((depth, d, N_GATE), BF16)],
        compiler_params=_cparams(("parallel", "parallel")),
        name="regroup_w_in",
    )(w_in)


def _prep_layer(p, l, w_proj, w_gate):
    row = lambda a: a[l].reshape(1, -1).astype(F32)
    lw = {
        'layer': l,
        'norm1_g': row(p['norm1_g']),
        'w_proj': w_proj,
        'w_gate': w_gate,
        'fox_qg': jnp.tile(p['fox_qnorm_g'][l], N_HEADS).reshape(1, -1),
        'fox_kg': jnp.tile(p['fox_knorm_g'][l], N_HEADS).reshape(1, -1),
        'fox_fb': jnp.pad(p['fox_fbias'][l], (0, LANE - N_HEADS)).reshape(1, -1),
        'head_ones': _head_ones(),
        'lru_conv_w': p['lru_conv_w'][l],
        'lru_conv_b': row(p['lru_conv_b']),
        'lru_wa_bd': _block_diag(p['lru_wa'][l]).astype(BF16),
        'lru_ba': row(p['lru_ba']),
        'lru_wi_bd': _block_diag(p['lru_wi'][l]).astype(BF16),
        'lru_bi': row(p['lru_bi']),
        'lru_lambda': row(p['lru_lambda']),
        'rw_mu': row(p['rw_mu']),
        'rw_w0': row(p['rw_w0']),
        'rw_a0': row(p['rw_a0']),
        'rw_w2p': jnp.pad(p['rw_w2'][l], ((0, 96), (0, 0))).astype(BF16),
        'rw_a2p': jnp.pad(p['rw_a2'][l], ((32, 64), (0, 0))).astype(BF16),
        'rw_g2p': jnp.pad(p['rw_g2'][l], ((64, 0), (0, 0))).astype(BF16),
        'rw_kk': row(p['rw_kk']),
        'rw_ka': row(p['rw_ka']),
        'rw_rk': row(p['rw_rk']),
        'rw_ln_g': row(p['rw_ln_g']),
        'rw_ln_b': row(p['rw_ln_b']),
        'w_branch': p['w_branch'][l].astype(BF16),
        'w_out': p['w_out'][l].astype(BF16),
        'norm2_g': row(p['norm2_g']),
        'moe_w1': p['moe_w1'][l].astype(BF16),
        'moe_w3': p['moe_w3'][l].astype(BF16),
        'moe_w2': p['moe_w2'][l].astype(BF16),
    }
    w_r = jnp.concatenate([p['moe_we'][l], p['moe_wg'][l]], axis=1)
    lw['moe_wr'] = jnp.pad(w_r, ((0, 0), (0, LANE - w_r.shape[1])))
    b_r = jnp.concatenate([p['moe_be'][l], p['moe_bg'][l]])
    lw['moe_br'] = jnp.pad(b_r, (0, LANE - b_r.shape[0])).reshape(1, -1)
    return lw


def _rwkv_kernel(pd_ref, shift0_ref, st0_ref, mu_ref, w0_ref, w2_ref, a0_ref, a2_ref, g2_ref,
                 kkw_ref, ka_ref, rk_ref, lng_ref, lnb_ref, hs_ref,
                 y_ref, stn_ref, prev_sc, st_sc, *, t_valid):
    j = pl.program_id(1)
    W = BRANCH_W
    nb, C = pd_ref.shape[0], pd_ref.shape[1]
    R4 = N_HEADS * C

    @pl.when(j == 0)
    def _():
        prev_sc[...] = shift0_ref[...]
        st_sc[...] = st0_ref[...]

    hs = hs_ref[...]
    row = _row_iota((C, 1))
    live = (row + j * C) < t_valid
    rr = _row_iota((R4, R4))
    cc = _col_iota((R4, R4))
    same_head = (rr // C) == (cc // C)
    strict = same_head & ((cc % C) < (rr % C))
    incl = same_head & ((cc % C) <= (rr % C))
    eye = rr == cc
    tile4 = lambda x: jnp.concatenate([x] * N_HEADS, axis=0)
    stack = lambda x: jnp.concatenate([x[:, h * HEAD_DIM:(h + 1) * HEAD_DIM] for h in range(N_HEADS)], axis=0)

    def chain(bi):
        pd = pd_ref[bi]
        prv = jnp.where(row == 0, prev_sc[bi], pltpu.roll(pd, 1, 0))
        ps = pd + (prv - pd) * mu_ref[...]
        r = ps[:, 0:W]
        k = ps[:, W:2 * W]
        v = ps[:, 2 * W:3 * W]
        tail = ps[:, 3 * W:3 * W + RW_TAIL]
        w_log = -_softplus(-(w0_ref[...] + _dot(jnp.tanh(tail), w2_ref[...]))) - 0.5
        logw = -jnp.exp(w_log)
        a = _sigmoid(a0_ref[...] + _dot(tail, a2_ref[...]))
        g = _dot(_sigmoid(tail), g2_ref[...])
        kk = k * kkw_ref[...]
        kk_ss = _dot2(kk * kk, hs)
        yield
        kk = kk / jnp.maximum(jnp.sqrt(kk_ss), 1e-12)
        k = k * (1.0 + (a - 1.0) * ka_ref[...])
        logw = jnp.where(live, logw, 0.0)
        kk = jnp.where(live, kk, 0.0)
        kl = jnp.where(live, k, 0.0)

        c = _cumsum_rows(logw)
        e_pos = jnp.exp(c)
        e_neg = jnp.exp(-c)
        e_end = e_pos[C - 1:C, :]
        a_bd = jnp.where(same_head, tile4(-kk * jnp.exp(c - logw)), 0.0)
        r_bd = jnp.where(same_head, tile4(r * e_pos), 0.0)
        b_t = kk * a * e_neg
        k_t = kl * e_neg
        ar = jnp.concatenate([a_bd, r_bd], axis=0).astype(BF16)
        bk = jnp.concatenate([tile4(b_t), tile4(k_t)], axis=0).astype(BF16)
        m = _dot_nt(ar, bk)
        st = st_sc[bi]
        ars = _dot(ar, st)
        v_st = stack(v)
        yield
        L = jnp.where(strict, m[0:R4, 0:R4], 0.0)
        X = ars[0:R4] + _dot(jnp.where(strict, m[0:R4, R4:2 * R4], 0.0), v_st)
        yield
        s = 1
        while s < C:
            X = X + _dot(L, X)
            s *= 2
            if s < C:
                L = _dot(L, L)
            yield
        y_st = (ars[R4:2 * R4] + _dot(jnp.where(incl, m[R4:2 * R4, 0:R4], 0.0), X)
                + _dot(jnp.where(incl, m[R4:2 * R4, R4:2 * R4], 0.0), v_st))
        bg_bd = jnp.where(same_head, tile4(b_t * e_end), 0.0)
        kg_bd = jnp.where(same_head, tile4(k_t * e_end), 0.0)
        e_col = jnp.sum(jnp.where(eye, jnp.broadcast_to(e_end, (R4, R4)), 0.0), axis=1, keepdims=True)
        st_sc[bi] = st * e_col + _dot_tn(bg_bd, X) + _dot_tn(kg_bd, v_st)
        yield

        y = jnp.concatenate([y_st[h * C:(h + 1) * C] for h in range(N_HEADS)], axis=1)
        inv_d = 1.0 / HEAD_DIM
        mean = _dot2(y, hs) * inv_d
        dv = y - mean
        var = _dot2(dv * dv, hs) * inv_d
        yn = dv * lax.rsqrt(var + RW_GN_EPS) * lng_ref[...] + lnb_ref[...]
        bonus = _dot2(r * k * rk_ref[...], hs) * v
        y_ref[bi] = (yn + bonus) * g
        prev_sc[bi] = pd[C - 1:C, :]

    chains = [chain(bi) for bi in range(nb)]
    while chains:
        chains = [ch for ch in chains if next(ch, chains) is not chains]

    @pl.when(j == pl.num_programs(1) - 1)
    def _():
        stn_ref[...] = st_sc[...]


def _rwkv(pd, shift0, s0, lw, t_valid, nb):
    b, tp, _ = pd.shape
    W = BRANCH_W
    C = RW_CHUNK
    st0 = jnp.swapaxes(s0, 2, 3).reshape(b, W, HEAD_DIM)
    full = lambda a: pl.BlockSpec(a.shape, lambda i, j: (0,) * a.ndim)
    ws = (lw['rw_mu'], lw['rw_w0'], lw['rw_w2p'], lw['rw_a0'], lw['rw_a2p'], lw['rw_g2p'],
          lw['rw_kk'], lw['rw_ka'], lw['rw_rk'], lw['rw_ln_g'], lw['rw_ln_b'], lw['head_ones'])
    y, stn = pl.pallas_call(
        functools.partial(_rwkv_kernel, t_valid=t_valid),
        grid=(b // nb, tp // C),
        in_specs=[pl.BlockSpec((nb, C, D_RW_IN), lambda i, j: (i, j, 0)),
                  pl.BlockSpec((nb, 1, D_RW_IN), lambda i, j: (i, 0, 0)),
                  pl.BlockSpec((nb, W, HEAD_DIM), lambda i, j: (i, 0, 0))] + [full(a) for a in ws],
        out_specs=[pl.BlockSpec((nb, C, W), lambda i, j: (i, j, 0)),
                   pl.BlockSpec((nb, W, HEAD_DIM), lambda i, j: (i, 0, 0))],
        out_shape=[jax.ShapeDtypeStruct((b, tp, W), F32),
                   jax.ShapeDtypeStruct((b, W, HEAD_DIM), F32)],
        scratch_shapes=[pltpu.VMEM((nb, 1, D_RW_IN), F32), pltpu.VMEM((nb, W, HEAD_DIM), F32)],
        compiler_params=_cparams(("parallel", "arbitrary")),
        name="rwkv7",
    )(pd, shift0, st0, *ws)
    return y, jnp.swapaxes(stn.reshape(b, N_HEADS, HEAD_DIM, HEAD_DIM), 2, 3)


ATT_BLK = 256


def _suffix_ones(n):
    i = jnp.arange(n)
    return jnp.concatenate([(i[:, None] > i[None, :]), jnp.ones((n, n), bool)], axis=1).astype(BF16)


def _sb_prompt_kernel(q_ref, k_ref, v_ref, m_ref, o_ref, run_sc, acc_sc):
    i = pl.program_id(1)
    T = ATT_BLK
    msuf = m_ref[...]
    lower = _col_iota((T, T)) < _row_iota((T, T))
    heads = [slice(h * HEAD_DIM, (h + 1) * HEAD_DIM) for h in range(N_HEADS)]
    run_sc[...] = jnp.zeros_like(run_sc)
    acc_sc[...] = jnp.zeros_like(acc_sc)

    def block(kj, diag):
        start = pl.multiple_of(kj * T, T)
        zs = [_dot_nt(q_ref[0, :, sl], k_ref[0, pl.ds(start, T), sl]) for sl in heads]
        log_betas = [jnp.minimum(z, 0.0) - jnp.log(1.0 + jnp.exp(-jnp.abs(z))) for z in zs]
        lss = [lb - z for lb, z in zip(log_betas, zs)]
        if diag:
            lss = [jnp.where(lower, ls, 0.0) for ls in lss]
        sts = [_dot2(ls, msuf) for ls in lss]
        ws = [jnp.exp(lb + st + run_sc[h]) for h, (lb, st) in enumerate(zip(log_betas, sts))]
        if diag:
            ws = [jnp.where(lower, w, 0.0) for w in ws]
        for h, sl in enumerate(heads):
            acc_sc[h] += _dot(ws[h], v_ref[0, pl.ds(start, T), sl])
            run_sc[h] += sts[h][:, 0:1] + lss[h][:, 0:1]

    block(i, True)

    def body(jj, carry):
        block(i - 1 - jj, False)
        return carry

    lax.fori_loop(0, i, body, 0)
    o_ref[0] = jnp.concatenate([acc_sc[h] for h in range(N_HEADS)], axis=1)


def _sb_prompt(q, k, v):
    b, t, W = q.shape
    msuf = _suffix_ones(ATT_BLK)[:, :ATT_BLK]
    return pl.pallas_call(
        _sb_prompt_kernel,
        grid=(b, t // ATT_BLK),
        in_specs=[pl.BlockSpec((1, ATT_BLK, W), lambda i, j: (i, j, 0)),
                  pl.BlockSpec((1, t, W), lambda i, j: (i, 0, 0)),
                  pl.BlockSpec((1, t, W), lambda i, j: (i, 0, 0)),
                  pl.BlockSpec(msuf.shape, lambda i, j: (0, 0))],
        out_specs=pl.BlockSpec((1, ATT_BLK, W), lambda i, j: (i, j, 0)),
        out_shape=jax.ShapeDtypeStruct((b, t, W), F32),
        scratch_shapes=[pltpu.VMEM((N_HEADS, ATT_BLK, 1), F32), pltpu.VMEM((N_HEADS, ATT_BLK, HEAD_DIM), F32)],
        compiler_params=_cparams(("parallel", "arbitrary")),
        name="sb_prompt",
    )(q, k, v, msuf)


def _fox_prompt_kernel(q_ref, k_ref, v_ref, fq_ref, fk_ref, o_ref, m_sc, acc_sc):
    i = pl.program_id(1)
    T = ATT_BLK
    lower = _col_iota((T, T)) <= _row_iota((T, T))
    heads = [slice(h * HEAD_DIM, (h + 1) * HEAD_DIM) for h in range(N_HEADS)]
    m_sc[...] = jnp.full_like(m_sc, -1e30)
    acc_sc[...] = jnp.zeros_like(acc_sc)

    def block(kj, diag):
        start = pl.multiple_of(kj * T, T)
        ss = [_dot_nt(q_ref[0, :, sl], k_ref[0, pl.ds(start, T), sl])
              + (fq_ref[0, :, h:h + 1] - fk_ref[0, h:h + 1, pl.ds(start, T)]) for h, sl in enumerate(heads)]
        if diag:
            ss = [jnp.where(lower, s, -jnp.inf) for s in ss]
        m_olds = [m_sc[h] for h in range(N_HEADS)]
        m_news = [jnp.maximum(m, jnp.max(s, axis=1, keepdims=True)) for m, s in zip(m_olds, ss)]
        ps = [jnp.exp(s - m) for s, m in zip(ss, m_news)]
        for h in range(N_HEADS):
            alpha = jnp.exp(m_olds[h] - m_news[h])
            acc_sc[h] = acc_sc[h] * alpha + _dot(ps[h], v_ref[0, pl.ds(start, T), h * LANE:(h + 1) * LANE])
            m_sc[h] = m_news[h]

    block(i, True)

    def body(jj, carry):
        block(i - 1 - jj, False)
        return carry

    lax.fori_loop(0, i, body, 0)
    o_ref[0] = jnp.concatenate([acc_sc[h][:, 0:HEAD_DIM] / acc_sc[h][:, HEAD_DIM:LANE] for h in range(N_HEADS)],
                               axis=1)


def _fox_prompt(q, k, v, f_col, f_row):
    b, t, W = q.shape
    v4 = v.reshape(b, t, N_HEADS, HEAD_DIM)
    v_ones = jnp.concatenate([v4, jnp.ones_like(v4)], axis=3).reshape(b, t, N_HEADS * LANE)
    return pl.pallas_call(
        _fox_prompt_kernel,
        grid=(b, t // ATT_BLK),
        in_specs=[pl.BlockSpec((1, ATT_BLK, W), lambda i, j: (i, j, 0)),
                  pl.BlockSpec((1, t, W), lambda i, j: (i, 0, 0)),
                  pl.BlockSpec((1, t, N_HEADS * LANE), lambda i, j: (i, 0, 0)),
                  pl.BlockSpec((1, ATT_BLK, 8), lambda i, j: (i, j, 0)),
                  pl.BlockSpec((1, 8, t), lambda i, j: (i, 0, 0))],
        out_specs=pl.BlockSpec((1, ATT_BLK, W), lambda i, j: (i, j, 0)),
        out_shape=jax.ShapeDtypeStruct((b, t, W), F32),
        scratch_shapes=[pltpu.VMEM((N_HEADS, ATT_BLK, 1), F32), pltpu.VMEM((N_HEADS, ATT_BLK, LANE), F32)],
        compiler_params=_cparams(("parallel", "arbitrary")),
        name="fox_prompt",
    )(q, k, v_ones, f_col, f_row)


def _cumsum_kernel(x_ref, o_ref):
    n = x_ref.shape[2] // LANE
    parts = [_cumsum_lanes(x_ref[0, :, c * LANE:(c + 1) * LANE]) for c in range(n)]
    carry = jnp.zeros((8, LANE), F32)
    for c, s in enumerate(parts):
        o_ref[0, :, c * LANE:(c + 1) * LANE] = s + carry
        carry = carry + jnp.broadcast_to(s[:, LANE - 1:LANE], s.shape)


def _cumsum_prompt(logf_row):
    b, _, t = logf_row.shape
    return pl.pallas_call(
        _cumsum_kernel,
        grid=(b,),
        in_specs=[pl.BlockSpec((1, 8, t), lambda i: (i, 0, 0))],
        out_specs=pl.BlockSpec((1, 8, t), lambda i: (i, 0, 0)),
        out_shape=jax.ShapeDtypeStruct((b, 8, t), F32),
        compiler_params=_cparams(("parallel",)),
        name="logf_cumsum",
    )(logf_row)


PAGES_PER_STEP = 8


QROWS = N_HEADS * 8


def _f_sample_kernel(pt_ref, pool_ref, new_ref, o_ref, x_sc):
    i = pl.program_id(0)
    n_pages = pt_ref.shape[1]
    n = n_pages + 1

    def gather(j, carry):
        p = pt_ref[i, j]
        tile = pool_ref[0, p // 2]
        x_sc[j] = jnp.where(p % 2 == 1, pltpu.roll(tile, N_HEADS, 0), tile)
        return carry

    lax.fori_loop(0, n_pages, gather, 0)
    x_sc[n_pages] = new_ref[0]
    c = _cumsum_lanes(x_sc[...])
    tot = jnp.broadcast_to(c[:, :, LANE - 1:LANE], c.shape)
    inc = tot
    s = 1
    while s < n:
        inc = inc + jnp.concatenate([jnp.zeros((s, 8, LANE), F32), inc[0:n - s]], axis=0)
        s *= 2
    o_ref[0] = c + (inc - tot)


def _f_sample(pool_pairs, logf_new, page_table, layer):
    b, n_pages = page_table.shape
    return pl.pallas_call(
        _f_sample_kernel,
        grid_spec=pltpu.PrefetchScalarGridSpec(
            num_scalar_prefetch=1,
            grid=(b,),
            in_specs=[pl.BlockSpec((1,) + pool_pairs.shape[1:], lambda i, pt: (layer, 0, 0, 0)),
                      pl.BlockSpec((1, 8, LANE), lambda i, pt: (i, 0, 0))],
            out_specs=pl.BlockSpec((1, n_pages + 1, 8, LANE), lambda i, pt: (i, 0, 0, 0)),
            scratch_shapes=[pltpu.VMEM((n_pages + 1, 8, LANE), F32)]),
        out_shape=jax.ShapeDtypeStruct((b, n_pages + 1, 8, LANE), F32),
        compiler_params=_cparams(("arbitrary",)),
        name="logf_cumsum_paged",
    )(page_table, pool_pairs, logf_new)


def _pages_t(cache):
    l, n, p, h, d = cache.shape
    return jnp.transpose(cache, (0, 1, 3, 4, 2)).reshape(l, n, h * d, p)


def _logf_pairs(cache_logf):
    l, n, p, h = cache_logf.shape
    return jnp.swapaxes(cache_logf, 2, 3).reshape(l, n // 2, 2 * h, p)


def _rows_t(x):
    return jnp.pad(jnp.swapaxes(x, 1, 2), ((0, 0), (0, 0), (0, LANE - x.shape[1])))


def _head_mask():
    return (jnp.arange(QROWS)[:, None] // 8) == (jnp.arange(BRANCH_W)[None, :] // HEAD_DIM)


def _heads_q(q):
    b, t, _ = q.shape
    q8 = jnp.pad(q, ((0, 0), (0, 8 - t), (0, 0)))
    return jnp.where(_head_mask(), jnp.tile(q8, (1, N_HEADS, 1)), 0).astype(BF16)


def _heads_out(o, t):
    b = o.shape[0]
    o4 = jnp.where(_head_mask(), o, 0.0).reshape(b, N_HEADS, 8, BRANCH_W)
    return jnp.sum(o4, axis=1)[:, :t, :]


def _sample_attn_kernel(pt_ref, qs_ref, qf_ref, fq_ref, ksn_ref, vsn_ref, kfn_ref, vfn_ref, f_ref, m_ref,
                        ck_sb, cv_sb, ck_fx, cv_fx, os_ref, of_ref,
                        kb_sb, vb_sb, kb_fx, vb_fx, sems, run_sc, accs_sc, m_sc, l_sc, accf_sc, *, layer, n_new):
    G = PAGES_PER_STEP
    b = pl.program_id(0)
    n_pages = pt_ref.shape[1]
    n_groups = n_pages // G
    caches = (ck_sb, cv_sb, ck_fx, cv_fx)
    bufs = (kb_sb, vb_sb, kb_fx, vb_fx)
    qs = qs_ref[0]
    qf = qf_ref[0]
    fq = fq_ref[0]
    msuf = m_ref[...]

    def page_of(grp, i):
        return n_pages - 1 - (grp * G + i)

    def copies(grp, slot):
        return [pltpu.make_async_copy(caches[a].at[layer, pt_ref[b, page_of(grp, i)]], bufs[a].at[slot, i],
                                      sems.at[slot, a])
                for a in range(len(caches)) for i in range(G)]

    def sb_blocks(kts, vts, mask):
        zs = [jnp.dot(qs, kt.astype(BF16), preferred_element_type=F32) for kt in kts]
        ls = [-_softplus(z) for z in zs]
        if mask is not None:
            ls = [jnp.where(mask, l, 0.0) for l in ls]
        st = _dot2(jnp.concatenate(ls, axis=0) if len(ls) > 1 else ls[0], msuf)
        run = run_sc[...]
        acc = accs_sc[...]
        for g in range(len(kts)):
            sg = st[g * QROWS:(g + 1) * QROWS]
            w = jnp.exp(zs[g] + ls[g] + sg[:, 0:LANE] + run)
            if mask is not None:
                w = jnp.where(mask, w, 0.0)
            acc = acc + _dot_nt(w, vts[g])
            run = run + sg[:, LANE:2 * LANE]
        run_sc[...] = run
        accs_sc[...] = acc

    def head_rows(f):
        return jnp.concatenate([jnp.broadcast_to(f[h:h + 1, :], (8, LANE)) for h in range(N_HEADS)], axis=0)

    def fox_blocks(kts, vts, fks, mask):
        ss = [jnp.dot(qf, kt.astype(BF16), preferred_element_type=F32) + (fq - head_rows(fk))
              for kt, fk in zip(kts, fks)]
        if mask is not None:
            ss = [jnp.where(mask, s, -jnp.inf) for s in ss]
        s_max = ss[0]
        for s in ss[1:]:
            s_max = jnp.maximum(s_max, s)
        m_old = m_sc[...]
        m_new = jnp.maximum(m_old, jnp.max(s_max, axis=1, keepdims=True))
        alpha = jnp.exp(m_old - m_new)
        ps = [jnp.exp(s - m_new) for s in ss]
        p_sum = ps[0]
        pv = _dot_nt(ps[0], vts[0])
        for p, vt in zip(ps[1:], vts[1:]):
            p_sum = p_sum + p
            pv = pv + _dot_nt(p, vt)
        l_sc[...] = l_sc[...] * alpha + jnp.sum(p_sum, axis=1, keepdims=True)
        accf_sc[...] = accf_sc[...] * alpha + pv
        m_sc[...] = m_new

    for c in copies(0, 0):
        c.start()

    run_sc[...] = jnp.zeros_like(run_sc)
    accs_sc[...] = jnp.zeros_like(accs_sc)
    m_sc[...] = jnp.full_like(m_sc, -1e30)
    l_sc[...] = jnp.zeros_like(l_sc)
    accf_sc[...] = jnp.zeros_like(accf_sc)
    t = _row_iota((QROWS, LANE)) % 8
    s = _col_iota((QROWS, LANE))
    sb_blocks([ksn_ref[0]], [vsn_ref[0]], (s < t) & (s < n_new))
    fox_blocks([kfn_ref[0]], [vfn_ref[0]], [f_ref[0, n_pages]], (s <= t) & (s < n_new))

    def group(grp, carry):
        slot = grp % 2

        @pl.when(grp + 1 < n_groups)
        def _():
            for c in copies(grp + 1, 1 - slot):
                c.start()

        for c in copies(grp, slot):
            c.wait()
        sb_blocks([kb_sb[slot, i] for i in range(G)], [vb_sb[slot, i] for i in range(G)], None)
        fox_blocks([kb_fx[slot, i] for i in range(G)], [vb_fx[slot, i] for i in range(G)],
                   [f_ref[0, page_of(grp, i)] for i in range(G)], None)
        return carry

    lax.fori_loop(0, n_groups, group, 0)
    os_ref[0] = accs_sc[...]
    of_ref[0] = accf_sc[...] / l_sc[...]


def _sample_attn(q_sb, k_sb, v_sb, q_fx, k_fx, v_fx, logf_new, caches, pool_pairs, page_table, layer):
    b, t, _ = q_sb.shape
    n_pages = page_table.shape[1]
    G = PAGES_PER_STEP
    lf = jnp.pad(jnp.swapaxes(logf_new, 1, 2), ((0, 0), (0, 8 - N_HEADS), (0, LANE - t)))
    f_all = _f_sample(pool_pairs, lf, page_table, layer)
    f_q = f_all[:, n_pages, :N_HEADS, :8]
    f_q = jnp.broadcast_to(f_q.reshape(b, QROWS, 1), (b, QROWS, LANE))
    msuf = _suffix_ones(LANE)
    q_spec = pl.BlockSpec((1, QROWS, BRANCH_W), lambda i, pt: (i, 0, 0))
    new_spec = pl.BlockSpec((1, BRANCH_W, LANE), lambda i, pt: (i, 0, 0))
    any_spec = pl.BlockSpec(memory_space=pl.ANY)
    page_buf = pltpu.VMEM((2, G, BRANCH_W, PAGE), F32)
    o_sb, o_fx = pl.pallas_call(
        functools.partial(_sample_attn_kernel, layer=layer, n_new=t),
        grid_spec=pltpu.PrefetchScalarGridSpec(
            num_scalar_prefetch=1,
            grid=(b,),
            in_specs=[q_spec, q_spec, pl.BlockSpec((1, QROWS, LANE), lambda i, pt: (i, 0, 0)),
                      new_spec, new_spec, new_spec, new_spec,
                      pl.BlockSpec((1, n_pages + 1, 8, LANE), lambda i, pt: (i, 0, 0, 0)),
                      pl.BlockSpec(msuf.shape, lambda i, pt: (0, 0)),
                      any_spec, any_spec, any_spec, any_spec],
            out_specs=[q_spec, q_spec],
            scratch_shapes=[page_buf, page_buf, page_buf, page_buf, pltpu.SemaphoreType.DMA((2, 4)),
                            pltpu.VMEM((QROWS, LANE), F32), pltpu.VMEM((QROWS, BRANCH_W), F32),
                            pltpu.VMEM((QROWS, 1), F32), pltpu.VMEM((QROWS, 1), F32),
                            pltpu.VMEM((QROWS, BRANCH_W), F32)]),
        out_shape=[jax.ShapeDtypeStruct((b, QROWS, BRANCH_W), F32)] * 2,
        compiler_params=_cparams(("arbitrary",)),
        name="sample_attn",
    )(page_table, _heads_q(q_sb), _heads_q(q_fx), f_q, _rows_t(k_sb), _rows_t(v_sb), _rows_t(k_fx), _rows_t(v_fx),
      f_all, msuf, *caches)
    return _heads_out(o_sb, t), _heads_out(o_fx, t)


def _merge_kernel(x_ref, ya_ref, yb_ref, yc_ref, yd_ref, g_ref, wg_ref, wb_ref, wo_ref, o_ref):
    x = x_ref[...]
    xn = _rmsnorm(x, g_ref[...]).astype(BF16)
    m = None
    for n, y_ref in enumerate((ya_ref, yb_ref, yc_ref, yd_ref)):
        gate = _sigmoid(jnp.dot(xn, wg_ref[:, n * D_MODEL:(n + 1) * D_MODEL], preferred_element_type=F32))
        term = gate * _dot(y_ref[...], wb_ref[n])
        m = term if m is None else m + term
    o_ref[...] = x + _dot(m, wo_ref[...])


def _merge(x2d, ys, lw, tm):
    n = x2d.shape[0]
    row = lambda w: pl.BlockSpec((tm, w), lambda i: (i, 0))
    full = lambda a: pl.BlockSpec(a.shape, lambda i: (0,) * a.ndim)
    ws = (lw['norm1_g'], lw['w_gate'], lw['w_branch'], lw['w_out'])
    layer = lw['layer']
    wg_spec = pl.BlockSpec((None, D_MODEL, N_GATE), lambda i: (layer, 0, 0))
    return pl.pallas_call(
        _merge_kernel,
        grid=(n // tm,),
        in_specs=[row(D_MODEL)] + [row(BRANCH_W)] * 4 + [full(ws[0]), wg_spec, full(ws[2]), full(ws[3])],
        out_specs=row(D_MODEL),
        out_shape=jax.ShapeDtypeStruct((n, D_MODEL), F32),
        compiler_params=_cparams(("parallel",)),
        name="merge",
    )(x2d, *ys, *ws)


def _router_gate(logits):
    lane = _col_iota(logits.shape)
    lane_f = lane.astype(F32)
    big = float(LANE)
    neg = -jnp.inf
    gl = jnp.where((lane >= N_EXPERTS) & (lane < N_EXPERTS + N_GROUPS), logits, neg)
    g_max = jnp.max(gl, axis=1, keepdims=True)
    g_idx = jnp.min(jnp.where(gl == g_max, lane_f, big), axis=1, keepdims=True) - float(N_EXPERTS)
    g_w = 1.0 / jnp.sum(jnp.exp(gl - g_max), axis=1, keepdims=True)
    per_group = N_EXPERTS // N_GROUPS
    lo = g_idx * float(per_group)
    el = jnp.where((lane_f >= lo) & (lane_f < lo + float(per_group)), logits, neg)
    m1 = jnp.max(el, axis=1, keepdims=True)
    i1 = jnp.min(jnp.where(el == m1, lane_f, big), axis=1, keepdims=True)
    el2 = jnp.where(lane_f == i1, neg, el)
    m2 = jnp.max(el2, axis=1, keepdims=True)
    i2 = jnp.min(jnp.where(el2 == m2, lane_f, big), axis=1, keepdims=True)
    d = jnp.exp(m2 - m1)
    w1 = g_w / (1.0 + d)
    return jnp.where(lane_f == i1, w1, 0.0) + jnp.where(lane_f == i2, w1 * d, 0.0)


def _moe_kernel(x_ref, g_ref, wr_ref, br_ref, w1_ref, w3_ref, w2_ref, o_ref, xn_sc, gate_sc, acc_sc):
    e = pl.program_id(1)

    @pl.when(e == 0)
    def _():
        x = x_ref[...]
        xn = _rmsnorm(x, g_ref[...])
        xn_sc[...] = xn.astype(BF16)
        logits = jnp.dot(xn, wr_ref[...], preferred_element_type=F32, precision=lax.Precision.HIGHEST) + br_ref[...]
        gate_sc[...] = _router_gate(logits)
        acc_sc[...] = x

    xn = xn_sc[...]
    a = jnp.dot(xn, w1_ref[0], preferred_element_type=F32)
    h = a * _sigmoid(a) * jnp.dot(xn, w3_ref[0], preferred_element_type=F32)
    ge = jnp.sum(jnp.where(_col_iota(gate_sc.shape) == e, gate_sc[...], 0.0), axis=1, keepdims=True)
    acc_sc[...] += _dot(h * ge, w2_ref[0])

    @pl.when(e == pl.num_programs(1) - 1)
    def _():
        o_ref[...] = acc_sc[...]


def _moe(x2d, lw, tm):
    n = x2d.shape[0]
    row = pl.BlockSpec((tm, D_MODEL), lambda i, e: (i, 0))
    full = lambda a: pl.BlockSpec(a.shape, lambda i, e: (0,) * a.ndim)
    return pl.pallas_call(
        _moe_kernel,
        grid=(n // tm, N_EXPERTS),
        in_specs=[row, full(lw['norm2_g']), full(lw['moe_wr']), full(lw['moe_br']),
                  pl.BlockSpec((1, D_MODEL, D_EXP), lambda i, e: (e, 0, 0)),
                  pl.BlockSpec((1, D_MODEL, D_EXP), lambda i, e: (e, 0, 0)),
                  pl.BlockSpec((1, D_EXP, D_MODEL), lambda i, e: (e, 0, 0))],
        out_specs=row,
        out_shape=jax.ShapeDtypeStruct((n, D_MODEL), F32),
        scratch_shapes=[pltpu.VMEM((tm, D_MODEL), BF16), pltpu.VMEM((tm, LANE), F32), pltpu.VMEM((tm, D_MODEL), F32)],
        compiler_params=_cparams(("parallel", "arbitrary")),
        name="moe",
    )(x2d, lw['norm2_g'], lw['moe_wr'], lw['moe_br'], lw['moe_w1'], lw['moe_w3'], lw['moe_w2'])


def _pad_rows(a, mult):
    t = a.shape[1]
    tp = -(-t // mult) * mult
    return a if tp == t else jnp.pad(a, ((0, 0), (0, tp - t), (0, 0)))


def _layer(x, lw, state, paged):
    b, t, _ = x.shape
    n = b * t
    W = BRANCH_W
    conv0, h0, shift0, wkv0 = state
    tm = min(n, 256)
    x2d = x.reshape(n, D_MODEL)
    (lru_xy, rw_in, sbq, sbk, sbv, sbkb, sbvb, fq, fk, fv, fkb, fvb, logf_pad) = _proj(x2d, lw, tm)
    r3 = lambda a: a.reshape(b, t, a.shape[-1])
    logf = r3(logf_pad[:, :N_HEADS])

    tt = 256 if t % 256 == 0 else 8
    y_a, conv_n, h_n = _lru(_pad_rows(r3(lru_xy), tt), jnp.pad(conv0, ((0, 0), (8 - (CONV_W - 1), 0), (0, 0))),
                            h0[:, None, :], lw, tt, t)
    rw3 = r3(rw_in)
    y_d, wkv_n = _rwkv(_pad_rows(rw3, RW_CHUNK), shift0[:, None, :], wkv0, lw, t, 4)

    if paged is None:
        y_b = _sb_prompt(r3(sbq), r3(sbkb), r3(sbvb))
        f_row = _cumsum_prompt(jnp.pad(jnp.swapaxes(logf, 1, 2), ((0, 0), (0, 8 - N_HEADS), (0, 0))))
        y_c = _fox_prompt(r3(fq), r3(fkb), r3(fvb), jnp.swapaxes(f_row, 1, 2), f_row)
    else:
        sb_kt, sb_vt, fox_kt, fox_vt, logf_pairs, page_table, layer = paged
        y_b, y_c = _sample_attn(r3(sbq), r3(sbk), r3(sbv), r3(fq), r3(fk), r3(fv), logf,
                                (sb_kt, sb_vt, fox_kt, fox_vt), logf_pairs, page_table, layer)

    ys = (y_a[:, :t].reshape(n, W), y_b.reshape(n, W), y_c.reshape(n, W), y_d[:, :t].reshape(n, W))
    x1 = _merge(x2d, ys, lw, tm)
    x2 = _moe(x1, lw, min(n, 1024))
    hd = lambda a: a.reshape(b, t, N_HEADS, HEAD_DIM)
    new_state = (hd(sbk), hd(sbv), hd(fk), hd(fv), logf, conv_n[:, 8 - (CONV_W - 1):], h_n[:, 0], rw3[:, t - 1], wkv_n)
    return x2.reshape(b, t, D_MODEL), new_state


def kernel(x_prompt, x_sample, cache_sb_k, cache_sb_v, cache_fox_k, cache_fox_v, cache_fox_logf, state_lru_conv, state_lru_h, state_rwkv_shift, state_rwkv_wkv, page_table, norm1_g, w_in, lru_conv_w, lru_conv_b, lru_wa, lru_ba, lru_wi, lru_bi, lru_lambda, fox_qnorm_g, fox_knorm_g, fox_fbias, rw_mu, rw_w0, rw_w2, rw_a0, rw_a2, rw_g2, rw_kk, rw_ka, rw_rk, rw_ln_g, rw_ln_b, w_branch, w_out, norm2_g, moe_wg, moe_bg, moe_we, moe_be, moe_w1, moe_w3, moe_w2):
    params = dict(norm1_g=norm1_g, w_in=w_in, lru_conv_w=lru_conv_w, lru_conv_b=lru_conv_b, lru_wa=lru_wa,
                  lru_ba=lru_ba, lru_wi=lru_wi, lru_bi=lru_bi, lru_lambda=lru_lambda, fox_qnorm_g=fox_qnorm_g,
                  fox_knorm_g=fox_knorm_g, fox_fbias=fox_fbias, rw_mu=rw_mu, rw_w0=rw_w0, rw_w2=rw_w2, rw_a0=rw_a0,
                  rw_a2=rw_a2, rw_g2=rw_g2, rw_kk=rw_kk, rw_ka=rw_ka, rw_rk=rw_rk, rw_ln_g=rw_ln_g, rw_ln_b=rw_ln_b,
                  w_branch=w_branch, w_out=w_out, norm2_g=norm2_g, moe_wg=moe_wg, moe_bg=moe_bg, moe_we=moe_we,
                  moe_be=moe_be, moe_w1=moe_w1, moe_w3=moe_w3, moe_w2=moe_w2)
    depth = w_in.shape[0]
    bp = x_prompt.shape[0]
    sb_kt, sb_vt = _pages_t(cache_sb_k), _pages_t(cache_sb_v)
    fox_kt, fox_vt = _pages_t(cache_fox_k), _pages_t(cache_fox_v)
    logf_pairs = _logf_pairs(cache_fox_logf)
    prompt_init = (jnp.zeros((bp, CONV_W - 1, BRANCH_W), F32), jnp.zeros((bp, BRANCH_W), F32),
                   jnp.zeros((bp, D_RW_IN), F32), jnp.zeros((bp, N_HEADS, HEAD_DIM, HEAD_DIM), F32))
    w_proj, w_gate = _regroup_w_in(w_in)
    y_p, y_s = x_prompt, x_sample
    st_p, st_s = [], []
    for l in range(depth):
        lw = _prep_layer(params, l, w_proj, w_gate)
        y_p, sp = _layer(y_p, lw, prompt_init, None)
        y_s, ss = _layer(y_s, lw, (state_lru_conv[l], state_lru_h[l], state_rwkv_shift[l], state_rwkv_wkv[l]),
                         (sb_kt, sb_vt, fox_kt, fox_vt, logf_pairs, page_table, l))
        st_p.append(sp)
        st_s.append(ss)
    stack = lambda sts: [jnp.stack([s[i] for s in sts], axis=0) for i in range(9)]
    return (y_p, y_s, *stack(st_p), *stack(st_s))
```

```python
import functools
import math

import jax
import jax.numpy as jnp
from jax import lax
from jax.experimental import pallas as pl
from jax.experimental.pallas import tpu as pltpu

F32 = jnp.float32
BF16 = jnp.bfloat16

D_MODEL = 1024
HEAD_DIM = 64
N_HEADS = 4
BRANCH_W = N_HEADS * HEAD_DIM
CONV_W = 4
LRU_C = 8.0
PAGE = 128
RW_TAIL = 128
D_RW_IN = 3 * BRANCH_W + RW_TAIL
N_MAIN = 8 * BRANCH_W
N_GROUPS = 4
N_EXPERTS = 16
D_EXP = 256
RMS_EPS = 1e-6
RW_GN_EPS = 64e-5
ATT_SCALE = HEAD_DIM ** -0.5
LANE = 128
RW_CHUNK = 64
VMEM_LIMIT = 56 * 1024 * 1024


def _cparams(sem):
    return pltpu.CompilerParams(dimension_semantics=sem, vmem_limit_bytes=VMEM_LIMIT)


def _dot(a, b):
    return jnp.dot(a.astype(BF16), b.astype(BF16), preferred_element_type=F32)


def _dot_nt(a, b):
    return lax.dot_general(a.astype(BF16), b.astype(BF16), (((1,), (1,)), ((), ())), preferred_element_type=F32)


def _dot_tn(a, b):
    return lax.dot_general(a.astype(BF16), b.astype(BF16), (((0,), (0,)), ((), ())), preferred_element_type=F32)


def _dot2(a, b_bf16):
    hi = a.astype(BF16)
    lo = (a - hi.astype(F32)).astype(BF16)
    return (jnp.dot(hi, b_bf16, preferred_element_type=F32) + jnp.dot(lo, b_bf16, preferred_element_type=F32))


def _softplus(x):
    return jnp.maximum(x, 0.0) + jnp.log1p(jnp.exp(-jnp.abs(x)))


def _sigmoid(x):
    return 1.0 / (1.0 + jnp.exp(-x))


def _gelu_tanh(x):
    return 0.5 * x * (1.0 + jnp.tanh(math.sqrt(2.0 / math.pi) * (x + 0.044715 * (x * x * x))))


def _rmsnorm(x, g):
    return x * lax.rsqrt(jnp.mean(x * x, axis=-1, keepdims=True) + RMS_EPS) * g


def _row_iota(shape):
    return lax.broadcasted_iota(jnp.int32, shape, 0)


def _col_iota(shape):
    return lax.broadcasted_iota(jnp.int32, shape, 1)


def _scan_rows(a, u):
    n = a.shape[0]
    row = _row_iota(a.shape)
    s = 1
    while s < n:
        valid = row >= s
        a_sh = pltpu.roll(a, s, 0)
        u_sh = pltpu.roll(u, s, 0)
        u = jnp.where(valid, a * u_sh + u, u)
        a = jnp.where(valid, a * a_sh, a)
        s *= 2
    return a, u


def _cumsum_rows(x):
    n = x.shape[0]
    row = _row_iota(x.shape)
    s = 1
    while s < n:
        x = x + jnp.where(row >= s, pltpu.roll(x, s, 0), 0.0)
        s *= 2
    return x


def _cumsum_lanes(x):
    ax = x.ndim - 1
    n = x.shape[ax]
    col = lax.broadcasted_iota(jnp.int32, x.shape, ax)
    s = 1
    while s < n:
        x = x + jnp.where(col >= s, pltpu.roll(x, s, ax), 0.0)
        s *= 2
    return x


def _head_ones():
    i = jnp.arange(BRANCH_W) // HEAD_DIM
    return (i[:, None] == i[None, :]).astype(BF16)


def _proj_kernel(x_ref, g_ref, w_ref, qg_ref, kg_ref, fb_ref, hs_ref,
                 lru_ref, rw_ref, sbq_ref, sbk_ref, sbv_ref, sbkb_ref, sbvb_ref,
                 fq_ref, fk_ref, fv_ref, fkb_ref, fvb_ref, logf_ref, *, state_t):
    xn = _rmsnorm(x_ref[...], g_ref[...]).astype(BF16)

    def seg(a, b):
        return jnp.dot(xn, w_ref[:, a:b], preferred_element_type=F32)

    def put_state(ref, val):
        if state_t:
            ref[0] = val.T
        else:
            ref[...] = val

    W = BRANCH_W
    lru_ref[...] = seg(0, 2 * W)
    sbq_ref[...] = (seg(2 * W, 3 * W) * ATT_SCALE).astype(BF16)
    k = seg(3 * W, 4 * W)
    put_state(sbk_ref, k)
    sbkb_ref[...] = k.astype(BF16)
    v = seg(4 * W, 5 * W)
    put_state(sbv_ref, v)
    sbvb_ref[...] = v.astype(BF16)
    hs = hs_ref[...]
    q = seg(5 * W, 6 * W)
    q = q * lax.rsqrt(_dot2(q * q, hs) * (1.0 / HEAD_DIM) + RMS_EPS) * qg_ref[...]
    fq_ref[...] = (q * ATT_SCALE).astype(BF16)
    k = seg(6 * W, 7 * W)
    k = k * lax.rsqrt(_dot2(k * k, hs) * (1.0 / HEAD_DIM) + RMS_EPS) * kg_ref[...]
    put_state(fk_ref, k)
    fkb_ref[...] = k.astype(BF16)
    v = seg(7 * W, 8 * W)
    put_state(fv_ref, v)
    ones = jnp.ones((v.shape[0], HEAD_DIM), F32)
    fvb_ref[...] = jnp.concatenate(
        [p for h in range(N_HEADS) for p in (v[:, h * HEAD_DIM:(h + 1) * HEAD_DIM], ones)], axis=1).astype(BF16)
    rw_ref[...] = seg(N_MAIN, N_MAIN + D_RW_IN)
    f = seg(N_MAIN + D_RW_IN, N_MAIN + D_RW_IN + LANE) + fb_ref[...]
    logf_ref[...] = -_softplus(-f)


def _proj(x2d, lw, tm, seq_len=None):
    n = x2d.shape[0]
    W = BRANCH_W
    row = lambda w: pl.BlockSpec((tm, w), lambda i: (i, 0))
    full = lambda a: pl.BlockSpec(a.shape, lambda i: (0,) * a.ndim)
    ins = (x2d, lw['norm1_g'], lw['w_proj'], lw['fox_qg'], lw['fox_kg'], lw['fox_fb'], lw['head_ones'])
    out_w = (2 * W, D_RW_IN, W, W, W, W, W, W, W, W, W, N_HEADS * LANE, LANE)
    out_dt = (F32, F32, BF16, F32, F32, BF16, BF16, BF16, F32, F32, BF16, BF16, F32)
    out_specs = [row(w) for w in out_w]
    out_shape = [jax.ShapeDtypeStruct((n, w), dt) for w, dt in zip(out_w, out_dt)]
    if seq_len is not None:
        per_seq = seq_len // tm
        for idx in (3, 4, 8, 9):
            out_specs[idx] = pl.BlockSpec((1, W, tm), lambda i: (i // per_seq, 0, i % per_seq))
            out_shape[idx] = jax.ShapeDtypeStruct((n // seq_len, W, seq_len), F32)
    layer = lw['layer']
    w_spec = pl.BlockSpec((None, D_MODEL, N_PROJ), lambda i: (layer, 0, 0))
    return pl.pallas_call(
        functools.partial(_proj_kernel, state_t=seq_len is not None),
        grid=(n // tm,),
        in_specs=[row(D_MODEL), full(ins[1]), w_spec] + [full(a) for a in ins[3:]],
        out_specs=out_specs,
        out_shape=out_shape,
        compiler_params=_cparams(("parallel",)),
        name="proj",
    )(*ins)


def _lru_kernel(xy_ref, conv0_ref, h0_ref, cw_ref, cb_ref, wa_ref, ba_ref, wi_ref, bi_ref, lam_ref,
                y_ref, convn_ref, hn_ref, tail_sc, h_sc, *, t_last):
    j = pl.program_id(1)
    W = BRANCH_W

    @pl.when(j == 0)
    def _():
        tail_sc[...] = conv0_ref[0]
        h_sc[...] = h0_ref[0]

    x = xy_ref[0, :, 0:W]
    y = xy_ref[0, :, W:2 * W]
    tt = x.shape[0]
    tail = tail_sc[...]
    row8 = _row_iota((8, W))
    cw = cw_ref[...]
    xc = cb_ref[...] + cw[CONV_W - 1:CONV_W, :] * x
    for kback in range(1, CONV_W):
        xs = pltpu.roll(x, kback, 0)
        head = jnp.where(row8 < kback, pltpu.roll(tail, kback, 0), xs[0:8])
        xs = head if tt == 8 else jnp.concatenate([head, xs[8:]], axis=0)
        xc = xc + cw[CONV_W - 1 - kback:CONV_W - kback, :] * xs
    r = _sigmoid(_dot(xc, wa_ref[...]) + ba_ref[...])
    ig = _sigmoid(_dot(xc, wi_ref[...]) + bi_ref[...])
    log_a = (-LRU_C) * r * _softplus(-lam_ref[...])
    a = jnp.exp(log_a)
    u = jnp.sqrt(-jnp.tanh(log_a) * (a * a + 1.0)) * (ig * xc)
    ap, hloc = _scan_rows(a, u)
    h = ap * h_sc[...] + hloc
    y_ref[0] = h * _gelu_tanh(y)
    h_sc[...] = h[tt - 1:tt, :]
    tail_sc[...] = x[tt - 8:tt, :]

    @pl.when(j == pl.num_programs(1) - 1)
    def _():
        hn_ref[0] = h[t_last - 1:t_last, :]
        convn_ref[0] = x[t_last - 8:t_last, :] if t_last >= 8 else jnp.where(
            row8 < 8 - t_last, pltpu.roll(tail, 8 - t_last, 0), pltpu.roll(x[0:8], 8 - t_last, 0))


def _lru(xy, conv0, h0, lw, tt, t_valid):
    b, tp, _ = xy.shape
    W = BRANCH_W
    nt = tp // tt
    t_last = t_valid - (nt - 1) * tt
    full = lambda a: pl.BlockSpec(a.shape, lambda i, j: (0,) * a.ndim)
    ws = (lw['lru_conv_w'], lw['lru_conv_b'], lw['lru_wa_bd'], lw['lru_ba'], lw['lru_wi_bd'], lw['lru_bi'], lw['lru_lambda'])
    return pl.pallas_call(
        functools.partial(_lru_kernel, t_last=t_last),
        grid=(b, nt),
        in_specs=[pl.BlockSpec((1, tt, 2 * W), lambda i, j: (i, j, 0)),
                  pl.BlockSpec((1, 8, W), lambda i, j: (i, 0, 0)),
                  pl.BlockSpec((1, 1, W), lambda i, j: (i, 0, 0))] + [full(a) for a in ws],
        out_specs=[pl.BlockSpec((1, tt, W), lambda i, j: (i, j, 0)),
                   pl.BlockSpec((1, 8, W), lambda i, j: (i, 0, 0)),
                   pl.BlockSpec((1, 1, W), lambda i, j: (i, 0, 0))],
        out_shape=[jax.ShapeDtypeStruct((b, tp, W), F32),
                   jax.ShapeDtypeStruct((b, 8, W), F32),
                   jax.ShapeDtypeStruct((b, 1, W), F32)],
        scratch_shapes=[pltpu.VMEM((8, W), F32), pltpu.VMEM((1, W), F32)],
        compiler_params=_cparams(("parallel", "arbitrary")),
        name="rglru",
    )(xy, conv0, h0, *ws)


def _block_diag(w):
    n, c, d = w.shape
    eye = jnp.eye(n, dtype=w.dtype)
    return (eye[:, None, :, None] * w[:, :, None, :]).reshape(n * c, n * d)


N_PROJ = N_MAIN + D_RW_IN + LANE
N_GATE = N_HEADS * D_MODEL


def _regroup_kernel(w_ref, wp_ref, wg_ref):
    o_f = N_MAIN
    o_rw = N_MAIN + N_HEADS
    o_g = o_rw + D_RW_IN
    rows = w_ref.shape[1]
    wp_ref[0, :, 0:o_f] = w_ref[0, :, 0:o_f].astype(BF16)
    wp_ref[0, :, o_f:o_f + D_RW_IN] = w_ref[0, :, o_rw:o_g].astype(BF16)
    f_cols = jnp.concatenate([w_ref[0, :, o_f:o_rw], jnp.zeros((rows, LANE - N_HEADS), F32)], axis=1)
    wp_ref[0, :, o_f + D_RW_IN:N_PROJ] = f_cols.astype(BF16)
    wg_ref[0] = w_ref[0, :, o_g:o_g + N_GATE].astype(BF16)


def _regroup_w_in(w_in):
    depth, d, n = w_in.shape
    rows = 256
    return pl.pallas_call(
        _regroup_kernel,
        grid=(depth, d // rows),
        in_specs=[pl.BlockSpec((1, rows, n), lambda l, i: (l, i, 0))],
        out_specs=[pl.BlockSpec((1, rows, N_PROJ), lambda l, i: (l, i, 0)),
                   pl.BlockSpec((1, rows, N_GATE), lambda l, i: (l, i, 0))],
        out_shape=[jax.ShapeDtypeStruct((depth, d, N_PROJ), BF16), jax.ShapeDtypeStruct((depth, d, N_GATE), BF16)],
        compiler_params=_cparams(("parallel", "parallel")),
        name="regroup_w_in",
    )(w_in)


def _prep_layer(p, l, w_proj, w_gate):
    row = lambda a: a[l].reshape(1, -1).astype(F32)
    lw = {
        'layer': l,
        'norm1_g': row(p['norm1_g']),
        'w_proj': w_proj,
        'w_gate': w_gate,
        'fox_qg': jnp.tile(p['fox_qnorm_g'][l], N_HEADS).reshape(1, -1),
        'fox_kg': jnp.tile(p['fox_knorm_g'][l], N_HEADS).reshape(1, -1),
        'fox_fb': jnp.pad(p['fox_fbias'][l], (0, LANE - N_HEADS)).reshape(1, -1),
        'head_ones': _head_ones(),
        'lru_conv_w': p['lru_conv_w'][l],
        'lru_conv_b': row(p['lru_conv_b']),
        'lru_wa_bd': _block_diag(p['lru_wa'][l]).astype(BF16),
        'lru_ba': row(p['lru_ba']),
        'lru_wi_bd': _block_diag(p['lru_wi'][l]).astype(BF16),
        'lru_bi': row(p['lru_bi']),
        'lru_lambda': row(p['lru_lambda']),
        'rw_mu': row(p['rw_mu']),
        'rw_w0': row(p['rw_w0']),
        'rw_a0': row(p['rw_a0']),
        'rw_w2p': jnp.pad(p['rw_w2'][l], ((0, 96), (0, 0))).astype(BF16),
        'rw_a2p': jnp.pad(p['rw_a2'][l], ((32, 64), (0, 0))).astype(BF16),
        'rw_g2p': jnp.pad(p['rw_g2'][l], ((64, 0), (0, 0))).astype(BF16),
        'rw_kk': row(p['rw_kk']),
        'rw_ka': row(p['rw_ka']),
        'rw_rk': row(p['rw_rk']),
        'rw_ln_g': row(p['rw_ln_g']),
        'rw_ln_b': row(p['rw_ln_b']),
        'w_branch': p['w_branch'][l].astype(BF16),
        'w_out': p['w_out'][l].astype(BF16),
        'norm2_g': row(p['norm2_g']),
        'moe_w1': p['moe_w1'][l].astype(BF16),
        'moe_w3': p['moe_w3'][l].astype(BF16),
        'moe_w2': p['moe_w2'][l].astype(BF16),
    }
    w_r = jnp.concatenate([p['moe_we'][l], p['moe_wg'][l]], axis=1)
    lw['moe_wr'] = jnp.pad(w_r, ((0, 0), (0, LANE - w_r.shape[1])))
    b_r = jnp.concatenate([p['moe_be'][l], p['moe_bg'][l]])
    lw['moe_br'] = jnp.pad(b_r, (0, LANE - b_r.shape[0])).reshape(1, -1)
    return lw


def _rwkv_kernel(pd_ref, shift0_ref, st0_ref, mu_ref, w0_ref, w2_ref, a0_ref, a2_ref, g2_ref,
                 kkw_ref, ka_ref, rk_ref, lng_ref, lnb_ref, hs_ref,
                 y_ref, stn_ref, prev_sc, st_sc, *, t_valid):
    j = pl.program_id(1)
    W = BRANCH_W
    nb, C = pd_ref.shape[0], pd_ref.shape[1]
    R4 = N_HEADS * C

    @pl.when(j == 0)
    def _():
        prev_sc[...] = shift0_ref[...]
        st_sc[...] = st0_ref[...]

    hs = hs_ref[...]
    row = _row_iota((C, 1))
    live = (row + j * C) < t_valid
    rr = _row_iota((R4, R4))
    cc = _col_iota((R4, R4))
    same_head = (rr // C) == (cc // C)
    strict = same_head & ((cc % C) < (rr % C))
    incl = same_head & ((cc % C) <= (rr % C))
    eye = rr == cc
    tile4 = lambda x: jnp.concatenate([x] * N_HEADS, axis=0)
    stack = lambda x: jnp.concatenate([x[:, h * HEAD_DIM:(h + 1) * HEAD_DIM] for h in range(N_HEADS)], axis=0)

    def chain(bi):
        pd = pd_ref[bi]
        prv = jnp.where(row == 0, prev_sc[bi], pltpu.roll(pd, 1, 0))
        ps = pd + (prv - pd) * mu_ref[...]
        r = ps[:, 0:W]
        k = ps[:, W:2 * W]
        v = ps[:, 2 * W:3 * W]
        tail = ps[:, 3 * W:3 * W + RW_TAIL]
        w_log = -_softplus(-(w0_ref[...] + _dot(jnp.tanh(tail), w2_ref[...]))) - 0.5
        logw = -jnp.exp(w_log)
        a = _sigmoid(a0_ref[...] + _dot(tail, a2_ref[...]))
        g = _dot(_sigmoid(tail), g2_ref[...])
        kk = k * kkw_ref[...]
        kk_ss = _dot2(kk * kk, hs)
        yield
        kk = kk / jnp.maximum(jnp.sqrt(kk_ss), 1e-12)
        k = k * (1.0 + (a - 1.0) * ka_ref[...])
        logw = jnp.where(live, logw, 0.0)
        kk = jnp.where(live, kk, 0.0)
        kl = jnp.where(live, k, 0.0)

        c = _cumsum_rows(logw)
        e_pos = jnp.exp(c)
        e_neg = jnp.exp(-c)
        e_end = e_pos[C - 1:C, :]
        a_bd = jnp.where(same_head, tile4(-kk * jnp.exp(c - logw)), 0.0)
        r_bd = jnp.where(same_head, tile4(r * e_pos), 0.0)
        b_t = kk * a * e_neg
        k_t = kl * e_neg
        ar = jnp.concatenate([a_bd, r_bd], axis=0).astype(BF16)
        bk = jnp.concatenate([tile4(b_t), tile4(k_t)], axis=0).astype(BF16)
        m = _dot_nt(ar, bk)
        st = st_sc[bi]
        ars = _dot(ar, st)
        v_st = stack(v)
        yield
        L = jnp.where(strict, m[0:R4, 0:R4], 0.0)
        X = ars[0:R4] + _dot(jnp.where(strict, m[0:R4, R4:2 * R4], 0.0), v_st)
        yield
        s = 1
        while s < C:
            X = X + _dot(L, X)
            s *= 2
            if s < C:
                L = _dot(L, L)
            yield
        y_st = (ars[R4:2 * R4] + _dot(jnp.where(incl, m[R4:2 * R4, 0:R4], 0.0), X)
                + _dot(jnp.where(incl, m[R4:2 * R4, R4:2 * R4], 0.0), v_st))
        bg_bd = jnp.where(same_head, tile4(b_t * e_end), 0.0)
        kg_bd = jnp.where(same_head, tile4(k_t * e_end), 0.0)
        e_col = jnp.sum(jnp.where(eye, jnp.broadcast_to(e_end, (R4, R4)), 0.0), axis=1, keepdims=True)
        st_sc[bi] = st * e_col + _dot_tn(bg_bd, X) + _dot_tn(kg_bd, v_st)
        yield

        y = jnp.concatenate([y_st[h * C:(h + 1) * C] for h in range(N_HEADS)], axis=1)
        inv_d = 1.0 / HEAD_DIM
        mean = _dot2(y, hs) * inv_d
        dv = y - mean
        var = _dot2(dv * dv, hs) * inv_d
        yn = dv * lax.rsqrt(var + RW_GN_EPS) * lng_ref[...] + lnb_ref[...]
        bonus = _dot2(r * k * rk_ref[...], hs) * v
        y_ref[bi] = (yn + bonus) * g
        prev_sc[bi] = pd[C - 1:C, :]

    chains = [chain(bi) for bi in range(nb)]
    while chains:
        chains = [ch for ch in chains if next(ch, chains) is not chains]

    @pl.when(j == pl.num_programs(1) - 1)
    def _():
        stn_ref[...] = st_sc[...]


def _rwkv(pd, shift0, s0, lw, t_valid, nb):
    b, tp, _ = pd.shape
    W = BRANCH_W
    C = RW_CHUNK
    st0 = jnp.swapaxes(s0, 2, 3).reshape(b, W, HEAD_DIM)
    full = lambda a: pl.BlockSpec(a.shape, lambda i, j: (0,) * a.ndim)
    ws = (lw['rw_mu'], lw['rw_w0'], lw['rw_w2p'], lw['rw_a0'], lw['rw_a2p'], lw['rw_g2p'],
          lw['rw_kk'], lw['rw_ka'], lw['rw_rk'], lw['rw_ln_g'], lw['rw_ln_b'], lw['head_ones'])
    y, stn = pl.pallas_call(
        functools.partial(_rwkv_kernel, t_valid=t_valid),
        grid=(b // nb, tp // C),
        in_specs=[pl.BlockSpec((nb, C, D_RW_IN), lambda i, j: (i, j, 0)),
                  pl.BlockSpec((nb, 1, D_RW_IN), lambda i, j: (i, 0, 0)),
                  pl.BlockSpec((nb, W, HEAD_DIM), lambda i, j: (i, 0, 0))] + [full(a) for a in ws],
        out_specs=[pl.BlockSpec((nb, C, W), lambda i, j: (i, j, 0)),
                   pl.BlockSpec((nb, W, HEAD_DIM), lambda i, j: (i, 0, 0))],
        out_shape=[jax.ShapeDtypeStruct((b, tp, W), F32),
                   jax.ShapeDtypeStruct((b, W, HEAD_DIM), F32)],
        scratch_shapes=[pltpu.VMEM((nb, 1, D_RW_IN), F32), pltpu.VMEM((nb, W, HEAD_DIM), F32)],
        compiler_params=_cparams(("parallel", "arbitrary")),
        name="rwkv7",
    )(pd, shift0, st0, *ws)
    return y, jnp.swapaxes(stn.reshape(b, N_HEADS, HEAD_DIM, HEAD_DIM), 2, 3)


ATT_BLK = 256


def _suffix_ones(n):
    i = jnp.arange(n)
    return jnp.concatenate([(i[:, None] > i[None, :]), jnp.ones((n, n), bool)], axis=1).astype(BF16)


def _sb_prompt_kernel(q_ref, k_ref, v_ref, m_ref, o_ref, run_sc, acc_sc):
    i = pl.program_id(1)
    T = ATT_BLK
    msuf = m_ref[...]
    lower = _col_iota((T, T)) < _row_iota((T, T))
    heads = [slice(h * HEAD_DIM, (h + 1) * HEAD_DIM) for h in range(N_HEADS)]
    run_sc[...] = jnp.zeros_like(run_sc)
    acc_sc[...] = jnp.zeros_like(acc_sc)

    def block(kj, diag):
        start = pl.multiple_of(kj * T, T)
        zs = [_dot_nt(q_ref[0, :, sl], k_ref[0, pl.ds(start, T), sl]) for sl in heads]
        log_betas = [jnp.minimum(z, 0.0) - jnp.log(1.0 + jnp.exp(-jnp.abs(z))) for z in zs]
        lss = [lb - z for lb, z in zip(log_betas, zs)]
        if diag:
            lss = [jnp.where(lower, ls, 0.0) for ls in lss]
        sts = [_dot2(ls, msuf) for ls in lss]
        ws = [jnp.exp(lb + st + run_sc[h]) for h, (lb, st) in enumerate(zip(log_betas, sts))]
        if diag:
            ws = [jnp.where(lower, w, 0.0) for w in ws]
        for h, sl in enumerate(heads):
            acc_sc[h] += _dot(ws[h], v_ref[0, pl.ds(start, T), sl])
            run_sc[h] += sts[h][:, 0:1] + lss[h][:, 0:1]

    block(i, True)

    def body(jj, carry):
        block(i - 1 - jj, False)
        return carry

    lax.fori_loop(0, i, body, 0)
    o_ref[0] = jnp.concatenate([acc_sc[h] for h in range(N_HEADS)], axis=1)


def _sb_prompt(q, k, v):
    b, t, W = q.shape
    msuf = _suffix_ones(ATT_BLK)[:, :ATT_BLK]
    return pl.pallas_call(
        _sb_prompt_kernel,
        grid=(b, t // ATT_BLK),
        in_specs=[pl.BlockSpec((1, ATT_BLK, W), lambda i, j: (i, j, 0)),
                  pl.BlockSpec((1, t, W), lambda i, j: (i, 0, 0)),
                  pl.BlockSpec((1, t, W), lambda i, j: (i, 0, 0)),
                  pl.BlockSpec(msuf.shape, lambda i, j: (0, 0))],
        out_specs=pl.BlockSpec((1, ATT_BLK, W), lambda i, j: (i, j, 0)),
        out_shape=jax.ShapeDtypeStruct((b, t, W), F32),
        scratch_shapes=[pltpu.VMEM((N_HEADS, ATT_BLK, 1), F32), pltpu.VMEM((N_HEADS, ATT_BLK, HEAD_DIM), F32)],
        compiler_params=_cparams(("parallel", "arbitrary")),
        name="sb_prompt",
    )(q, k, v, msuf)


def _fox_prompt_kernel(q_ref, k_ref, v_ref, fq_ref, fk_ref, o_ref, m_sc, acc_sc):
    i = pl.program_id(1)
    T = ATT_BLK
    lower = _col_iota((T, T)) <= _row_iota((T, T))
    heads = [slice(h * HEAD_DIM, (h + 1) * HEAD_DIM) for h in range(N_HEADS)]
    m_sc[...] = jnp.full_like(m_sc, -1e30)
    acc_sc[...] = jnp.zeros_like(acc_sc)

    def block(kj, diag):
        start = pl.multiple_of(kj * T, T)
        ss = [_dot_nt(q_ref[0, :, sl], k_ref[0, pl.ds(start, T), sl])
              + (fq_ref[0, :, h:h + 1] - fk_ref[0, h:h + 1, pl.ds(start, T)]) for h, sl in enumerate(heads)]
        if diag:
            ss = [jnp.where(lower, s, -jnp.inf) for s in ss]
        m_olds = [m_sc[h] for h in range(N_HEADS)]
        m_news = [jnp.maximum(m, jnp.max(s, axis=1, keepdims=True)) for m, s in zip(m_olds, ss)]
        ps = [jnp.exp(s - m) for s, m in zip(ss, m_news)]
        for h in range(N_HEADS):
            alpha = jnp.exp(m_olds[h] - m_news[h])
            acc_sc[h] = acc_sc[h] * alpha + _dot(ps[h], v_ref[0, pl.ds(start, T), h * LANE:(h + 1) * LANE])
            m_sc[h] = m_news[h]

    block(i, True)

    def body(jj, carry):
        block(i - 1 - jj, False)
        return carry

    lax.fori_loop(0, i, body, 0)
    o_ref[0] = jnp.concatenate([acc_sc[h][:, 0:HEAD_DIM] / acc_sc[h][:, HEAD_DIM:LANE] for h in range(N_HEADS)],
                               axis=1)


def _fox_prompt(q, k, v_ones, f_col, f_row):
    b, t, W = q.shape
    return pl.pallas_call(
        _fox_prompt_kernel,
        grid=(b, t // ATT_BLK),
        in_specs=[pl.BlockSpec((1, ATT_BLK, W), lambda i, j: (i, j, 0)),
                  pl.BlockSpec((1, t, W), lambda i, j: (i, 0, 0)),
                  pl.BlockSpec((1, t, N_HEADS * LANE), lambda i, j: (i, 0, 0)),
                  pl.BlockSpec((1, ATT_BLK, 8), lambda i, j: (i, j, 0)),
                  pl.BlockSpec((1, 8, t), lambda i, j: (i, 0, 0))],
        out_specs=pl.BlockSpec((1, ATT_BLK, W), lambda i, j: (i, j, 0)),
        out_shape=jax.ShapeDtypeStruct((b, t, W), F32),
        scratch_shapes=[pltpu.VMEM((N_HEADS, ATT_BLK, 1), F32), pltpu.VMEM((N_HEADS, ATT_BLK, LANE), F32)],
        compiler_params=_cparams(("parallel", "arbitrary")),
        name="fox_prompt",
    )(q, k, v_ones, f_col, f_row)


def _cumsum_kernel(x_ref, o_ref):
    n = x_ref.shape[2] // LANE
    parts = [_cumsum_lanes(x_ref[0, :, c * LANE:(c + 1) * LANE]) for c in range(n)]
    carry = jnp.zeros((8, LANE), F32)
    for c, s in enumerate(parts):
        o_ref[0, :, c * LANE:(c + 1) * LANE] = s + carry
        carry = carry + jnp.broadcast_to(s[:, LANE - 1:LANE], s.shape)


def _cumsum_prompt(logf_row):
    b, _, t = logf_row.shape
    return pl.pallas_call(
        _cumsum_kernel,
        grid=(b,),
        in_specs=[pl.BlockSpec((1, 8, t), lambda i: (i, 0, 0))],
        out_specs=pl.BlockSpec((1, 8, t), lambda i: (i, 0, 0)),
        out_shape=jax.ShapeDtypeStruct((b, 8, t), F32),
        compiler_params=_cparams(("parallel",)),
        name="logf_cumsum",
    )(logf_row)


PAGES_PER_STEP = 8
PAGE_SLOTS = 3
QROWS = N_HEADS * 8


def _f_sample_kernel(pt_ref, pool_ref, new_ref, o_ref, x_sc):
    i = pl.program_id(0)
    n_pages = pt_ref.shape[1]
    n = n_pages + 1

    def gather(j, carry):
        p = pt_ref[i, j]
        tile = pool_ref[0, p // 2]
        x_sc[j] = jnp.where(p % 2 == 1, pltpu.roll(tile, N_HEADS, 0), tile)
        return carry

    lax.fori_loop(0, n_pages, gather, 0)
    x_sc[n_pages] = new_ref[0]
    c = _cumsum_lanes(x_sc[...])
    tot = jnp.broadcast_to(c[:, :, LANE - 1:LANE], c.shape)
    inc = tot
    s = 1
    while s < n:
        inc = inc + jnp.concatenate([jnp.zeros((s, 8, LANE), F32), inc[0:n - s]], axis=0)
        s *= 2
    o_ref[0] = c + (inc - tot)


def _f_sample(pool_pairs, logf_new, page_table, layer):
    b, n_pages = page_table.shape
    return pl.pallas_call(
        _f_sample_kernel,
        grid_spec=pltpu.PrefetchScalarGridSpec(
            num_scalar_prefetch=1,
            grid=(b,),
            in_specs=[pl.BlockSpec((1,) + pool_pairs.shape[1:], lambda i, pt: (layer, 0, 0, 0)),
                      pl.BlockSpec((1, 8, LANE), lambda i, pt: (i, 0, 0))],
            out_specs=pl.BlockSpec((1, n_pages + 1, 8, LANE), lambda i, pt: (i, 0, 0, 0)),
            scratch_shapes=[pltpu.VMEM((n_pages + 1, 8, LANE), F32)]),
        out_shape=jax.ShapeDtypeStruct((b, n_pages + 1, 8, LANE), F32),
        compiler_params=_cparams(("arbitrary",)),
        name="logf_cumsum_paged",
    )(page_table, pool_pairs, logf_new)


def _pages_t(cache):
    l, n, p, h, d = cache.shape
    return jnp.transpose(cache, (0, 1, 3, 4, 2)).reshape(l, n, h * d, p)


def _logf_pairs(cache_logf):
    l, n, p, h = cache_logf.shape
    return jnp.swapaxes(cache_logf, 2, 3).reshape(l, n // 2, 2 * h, p)


def _rows_t(x):
    return jnp.pad(jnp.swapaxes(x, 1, 2), ((0, 0), (0, 0), (0, LANE - x.shape[1])))


def _head_mask():
    return (jnp.arange(QROWS)[:, None] // 8) == (jnp.arange(BRANCH_W)[None, :] // HEAD_DIM)


def _heads_q(q):
    b, t, _ = q.shape
    q8 = jnp.pad(q, ((0, 0), (0, 8 - t), (0, 0)))
    return jnp.where(_head_mask(), jnp.tile(q8, (1, N_HEADS, 1)), 0).astype(BF16)


def _heads_out(o, t):
    b = o.shape[0]
    o4 = jnp.where(_head_mask(), o, 0.0).reshape(b, N_HEADS, 8, BRANCH_W)
    return jnp.sum(o4, axis=1)[:, :t, :]


def _sample_attn_kernel(pt_ref, qs_ref, qf_ref, fq_ref, ksn_ref, vsn_ref, kfn_ref, vfn_ref, f_ref, m_ref,
                        ck_sb, cv_sb, ck_fx, cv_fx, os_ref, of_ref,
                        kb_sb, vb_sb, kb_fx, vb_fx, sems, run_sc, accs_sc, m_sc, l_sc, accf_sc, *, layer, n_new):
    G = PAGES_PER_STEP
    b = pl.program_id(0)
    n_pages = pt_ref.shape[1]
    n_groups = n_pages // G
    caches = (ck_sb, cv_sb, ck_fx, cv_fx)
    bufs = (kb_sb, vb_sb, kb_fx, vb_fx)
    qs = qs_ref[0]
    qf = qf_ref[0]
    fq = fq_ref[0]
    msuf = m_ref[...]

    def page_of(grp, i):
        return n_pages - 1 - (grp * G + i)

    def copies(grp, slot):
        return [pltpu.make_async_copy(caches[a].at[layer, pt_ref[b, page_of(grp, i)]], bufs[a].at[slot, i],
                                      sems.at[slot, a])
                for a in range(len(caches)) for i in range(G)]

    def sb_blocks(kts, vts, mask):
        zs = [jnp.dot(qs, kt.astype(BF16), preferred_element_type=F32) for kt in kts]
        ls = [-_softplus(z) for z in zs]
        if mask is not None:
            ls = [jnp.where(mask, l, 0.0) for l in ls]
        st = _dot2(jnp.concatenate(ls, axis=0) if len(ls) > 1 else ls[0], msuf)
        run = run_sc[...]
        acc = accs_sc[...]
        for g in range(len(kts)):
            sg = st[g * QROWS:(g + 1) * QROWS]
            w = jnp.exp(zs[g] + ls[g] + sg[:, 0:LANE] + run)
            if mask is not None:
                w = jnp.where(mask, w, 0.0)
            acc = acc + _dot_nt(w, vts[g])
            run = run + sg[:, LANE:2 * LANE]
        run_sc[...] = run
        accs_sc[...] = acc

    def head_rows(f):
        return jnp.concatenate([jnp.broadcast_to(f[h:h + 1, :], (8, LANE)) for h in range(N_HEADS)], axis=0)

    def fox_blocks(kts, vts, fks, mask):
        ss = [jnp.dot(qf, kt.astype(BF16), preferred_element_type=F32) + (fq - head_rows(fk))
              for kt, fk in zip(kts, fks)]
        if mask is not None:
            ss = [jnp.where(mask, s, -jnp.inf) for s in ss]
        s_max = ss[0]
        for s in ss[1:]:
            s_max = jnp.maximum(s_max, s)
        m_old = m_sc[...]
        m_new = jnp.maximum(m_old, jnp.max(s_max, axis=1, keepdims=True))
        alpha = jnp.exp(m_old - m_new)
        ps = [jnp.exp(s - m_new) for s in ss]
        p_sum = ps[0]
        pv = _dot_nt(ps[0], vts[0])
        for p, vt in zip(ps[1:], vts[1:]):
            p_sum = p_sum + p
            pv = pv + _dot_nt(p, vt)
        l_sc[...] = l_sc[...] * alpha + jnp.sum(p_sum, axis=1, keepdims=True)
        accf_sc[...] = accf_sc[...] * alpha + pv
        m_sc[...] = m_new

    for grp in range(PAGE_SLOTS - 1):
        for c in copies(grp, grp):
            c.start()

    run_sc[...] = jnp.zeros_like(run_sc)
    accs_sc[...] = jnp.zeros_like(accs_sc)
    m_sc[...] = jnp.full_like(m_sc, -1e30)
    l_sc[...] = jnp.zeros_like(l_sc)
    accf_sc[...] = jnp.zeros_like(accf_sc)
    t = _row_iota((QROWS, LANE)) % 8
    s = _col_iota((QROWS, LANE))
    sb_blocks([ksn_ref[0]], [vsn_ref[0]], (s < t) & (s < n_new))
    fox_blocks([kfn_ref[0]], [vfn_ref[0]], [f_ref[0, n_pages]], (s <= t) & (s < n_new))

    def group(grp, carry):
        slot = grp % PAGE_SLOTS
        ahead = grp + (PAGE_SLOTS - 1)

        @pl.when(ahead < n_groups)
        def _():
            for c in copies(ahead, ahead % PAGE_SLOTS):
                c.start()

        for c in copies(grp, slot):
            c.wait()
        sb_blocks([kb_sb[slot, i] for i in range(G)], [vb_sb[slot, i] for i in range(G)], None)
        fox_blocks([kb_fx[slot, i] for i in range(G)], [vb_fx[slot, i] for i in range(G)],
                   [f_ref[0, page_of(grp, i)] for i in range(G)], None)
        return carry

    lax.fori_loop(0, n_groups, group, 0)
    os_ref[0] = accs_sc[...]
    of_ref[0] = accf_sc[...] / l_sc[...]


def _sample_attn(q_sb, k_sb, v_sb, q_fx, k_fx, v_fx, logf_new, caches, pool_pairs, page_table, layer):
    b, t, _ = q_sb.shape
    n_pages = page_table.shape[1]
    G = PAGES_PER_STEP
    lf = jnp.pad(jnp.swapaxes(logf_new, 1, 2), ((0, 0), (0, 8 - N_HEADS), (0, LANE - t)))
    f_all = _f_sample(pool_pairs, lf, page_table, layer)
    f_q = f_all[:, n_pages, :N_HEADS, :8]
    f_q = jnp.broadcast_to(f_q.reshape(b, QROWS, 1), (b, QROWS, LANE))
    msuf = _suffix_ones(LANE)
    q_spec = pl.BlockSpec((1, QROWS, BRANCH_W), lambda i, pt: (i, 0, 0))
    new_spec = pl.BlockSpec((1, BRANCH_W, LANE), lambda i, pt: (i, 0, 0))
    any_spec = pl.BlockSpec(memory_space=pl.ANY)
    page_buf = pltpu.VMEM((PAGE_SLOTS, G, BRANCH_W, PAGE), F32)
    o_sb, o_fx = pl.pallas_call(
        functools.partial(_sample_attn_kernel, layer=layer, n_new=t),
        grid_spec=pltpu.PrefetchScalarGridSpec(
            num_scalar_prefetch=1,
            grid=(b,),
            in_specs=[q_spec, q_spec, pl.BlockSpec((1, QROWS, LANE), lambda i, pt: (i, 0, 0)),
                      new_spec, new_spec, new_spec, new_spec,
                      pl.BlockSpec((1, n_pages + 1, 8, LANE), lambda i, pt: (i, 0, 0, 0)),
                      pl.BlockSpec(msuf.shape, lambda i, pt: (0, 0)),
                      any_spec, any_spec, any_spec, any_spec],
            out_specs=[q_spec, q_spec],
            scratch_shapes=[page_buf, page_buf, page_buf, page_buf, pltpu.SemaphoreType.DMA((PAGE_SLOTS, 4)),
                            pltpu.VMEM((QROWS, LANE), F32), pltpu.VMEM((QROWS, BRANCH_W), F32),
                            pltpu.VMEM((QROWS, 1), F32), pltpu.VMEM((QROWS, 1), F32),
                            pltpu.VMEM((QROWS, BRANCH_W), F32)]),
        out_shape=[jax.ShapeDtypeStruct((b, QROWS, BRANCH_W), F32)] * 2,
        compiler_params=_cparams(("arbitrary",)),
        name="sample_attn",
    )(page_table, _heads_q(q_sb), _heads_q(q_fx), f_q, _rows_t(k_sb), _rows_t(v_sb), _rows_t(k_fx), _rows_t(v_fx),
      f_all, msuf, *caches)
    return _heads_out(o_sb, t), _heads_out(o_fx, t)


def _merge_kernel(x_ref, ya_ref, yb_ref, yc_ref, yd_ref, g_ref, wg_ref, wb_ref, wo_ref, o_ref):
    x = x_ref[...]
    xn = _rmsnorm(x, g_ref[...]).astype(BF16)
    m = None
    for n, y_ref in enumerate((ya_ref, yb_ref, yc_ref, yd_ref)):
        gate = _sigmoid(jnp.dot(xn, wg_ref[:, n * D_MODEL:(n + 1) * D_MODEL], preferred_element_type=F32))
        term = gate * _dot(y_ref[...], wb_ref[n])
        m = term if m is None else m + term
    o_ref[...] = x + _dot(m, wo_ref[...])


def _merge(x2d, ys, lw, tm):
    n = x2d.shape[0]
    row = lambda w: pl.BlockSpec((tm, w), lambda i: (i, 0))
    full = lambda a: pl.BlockSpec(a.shape, lambda i: (0,) * a.ndim)
    ws = (lw['norm1_g'], lw['w_gate'], lw['w_branch'], lw['w_out'])
    layer = lw['layer']
    wg_spec = pl.BlockSpec((None, D_MODEL, N_GATE), lambda i: (layer, 0, 0))
    return pl.pallas_call(
        _merge_kernel,
        grid=(n // tm,),
        in_specs=[row(D_MODEL)] + [row(BRANCH_W)] * 4 + [full(ws[0]), wg_spec, full(ws[2]), full(ws[3])],
        out_specs=row(D_MODEL),
        out_shape=jax.ShapeDtypeStruct((n, D_MODEL), F32),
        compiler_params=_cparams(("parallel",)),
        name="merge",
    )(x2d, *ys, *ws)


def _router_gate(logits):
    lane = _col_iota(logits.shape)
    lane_f = lane.astype(F32)
    big = float(LANE)
    neg = -jnp.inf
    gl = jnp.where((lane >= N_EXPERTS) & (lane < N_EXPERTS + N_GROUPS), logits, neg)
    g_max = jnp.max(gl, axis=1, keepdims=True)
    g_idx = jnp.min(jnp.where(gl == g_max, lane_f, big), axis=1, keepdims=True) - float(N_EXPERTS)
    g_w = 1.0 / jnp.sum(jnp.exp(gl - g_max), axis=1, keepdims=True)
    per_group = N_EXPERTS // N_GROUPS
    lo = g_idx * float(per_group)
    el = jnp.where((lane_f >= lo) & (lane_f < lo + float(per_group)), logits, neg)
    m1 = jnp.max(el, axis=1, keepdims=True)
    i1 = jnp.min(jnp.where(el == m1, lane_f, big), axis=1, keepdims=True)
    el2 = jnp.where(lane_f == i1, neg, el)
    m2 = jnp.max(el2, axis=1, keepdims=True)
    i2 = jnp.min(jnp.where(el2 == m2, lane_f, big), axis=1, keepdims=True)
    d = jnp.exp(m2 - m1)
    w1 = g_w / (1.0 + d)
    return jnp.where(lane_f == i1, w1, 0.0) + jnp.where(lane_f == i2, w1 * d, 0.0)


def _moe_kernel(x_ref, g_ref, wr_ref, br_ref, w1_ref, w3_ref, w2_ref, o_ref, xn_sc, gate_sc, acc_sc):
    e = pl.program_id(1)

    @pl.when(e == 0)
    def _():
        x = x_ref[...]
        xn = _rmsnorm(x, g_ref[...])
        xn_sc[...] = xn.astype(BF16)
        logits = jnp.dot(xn, wr_ref[...], preferred_element_type=F32, precision=lax.Precision.HIGHEST) + br_ref[...]
        gate_sc[...] = _router_gate(logits)
        acc_sc[...] = x

    xn = xn_sc[...]
    a = jnp.dot(xn, w1_ref[0], preferred_element_type=F32)
    h = a * _sigmoid(a) * jnp.dot(xn, w3_ref[0], preferred_element_type=F32)
    ge = jnp.sum(jnp.where(_col_iota(gate_sc.shape) == e, gate_sc[...], 0.0), axis=1, keepdims=True)
    acc_sc[...] += _dot(h * ge, w2_ref[0])

    @pl.when(e == pl.num_programs(1) - 1)
    def _():
        o_ref[...] = acc_sc[...]


def _moe(x2d, lw, tm):
    n = x2d.shape[0]
    row = pl.BlockSpec((tm, D_MODEL), lambda i, e: (i, 0))
    full = lambda a: pl.BlockSpec(a.shape, lambda i, e: (0,) * a.ndim)
    return pl.pallas_call(
        _moe_kernel,
        grid=(n // tm, N_EXPERTS),
        in_specs=[row, full(lw['norm2_g']), full(lw['moe_wr']), full(lw['moe_br']),
                  pl.BlockSpec((1, D_MODEL, D_EXP), lambda i, e: (e, 0, 0)),
                  pl.BlockSpec((1, D_MODEL, D_EXP), lambda i, e: (e, 0, 0)),
                  pl.BlockSpec((1, D_EXP, D_MODEL), lambda i, e: (e, 0, 0))],
        out_specs=row,
        out_shape=jax.ShapeDtypeStruct((n, D_MODEL), F32),
        scratch_shapes=[pltpu.VMEM((tm, D_MODEL), BF16), pltpu.VMEM((tm, LANE), F32), pltpu.VMEM((tm, D_MODEL), F32)],
        compiler_params=_cparams(("parallel", "arbitrary")),
        name="moe",
    )(x2d, lw['norm2_g'], lw['moe_wr'], lw['moe_br'], lw['moe_w1'], lw['moe_w3'], lw['moe_w2'])


def _pad_rows(a, mult):
    t = a.shape[1]
    tp = -(-t // mult) * mult
    return a if tp == t else jnp.pad(a, ((0, 0), (0, tp - t), (0, 0)))


def _layer(x, lw, state, paged):
    b, t, _ = x.shape
    n = b * t
    W = BRANCH_W
    conv0, h0, shift0, wkv0 = state
    tm = min(n, 256)
    x2d = x.reshape(n, D_MODEL)
    state_t = paged is None and t % tm == 0
    (lru_xy, rw_in, sbq, sbk, sbv, sbkb, sbvb, fq, fk, fv, fkb, fvb, logf_pad) = _proj(
        x2d, lw, tm, t if state_t else None)
    r3 = lambda a: a.reshape(b, t, a.shape[-1])
    logf = r3(logf_pad[:, :N_HEADS])

    tt = 256 if t % 256 == 0 else 8
    y_a, conv_n, h_n = _lru(_pad_rows(r3(lru_xy), tt), jnp.pad(conv0, ((0, 0), (8 - (CONV_W - 1), 0), (0, 0))),
                            h0[:, None, :], lw, tt, t)
    rw3 = r3(rw_in)
    y_d, wkv_n = _rwkv(_pad_rows(rw3, RW_CHUNK), shift0[:, None, :], wkv0, lw, t, 4)

    if paged is None:
        y_b = _sb_prompt(r3(sbq), r3(sbkb), r3(sbvb))
        f_row = _cumsum_prompt(jnp.pad(jnp.swapaxes(logf, 1, 2), ((0, 0), (0, 8 - N_HEADS), (0, 0))))
        y_c = _fox_prompt(r3(fq), r3(fkb), r3(fvb), jnp.swapaxes(f_row, 1, 2), f_row)
    else:
        sb_kt, sb_vt, fox_kt, fox_vt, logf_pairs, page_table, layer = paged
        y_b, y_c = _sample_attn(r3(sbq), r3(sbk), r3(sbv), r3(fq), r3(fk), r3(fv), logf,
                                (sb_kt, sb_vt, fox_kt, fox_vt), logf_pairs, page_table, layer)

    ys = (y_a[:, :t].reshape(n, W), y_b.reshape(n, W), y_c.reshape(n, W), y_d[:, :t].reshape(n, W))
    x1 = _merge(x2d, ys, lw, tm)
    x2 = _moe(x1, lw, min(n, 1024))
    if state_t:
        hd = lambda a: jnp.transpose(a.reshape(b, N_HEADS, HEAD_DIM, t), (0, 3, 1, 2))
    else:
        hd = lambda a: a.reshape(b, t, N_HEADS, HEAD_DIM)
    new_state = (hd(sbk), hd(sbv), hd(fk), hd(fv), logf, conv_n[:, 8 - (CONV_W - 1):], h_n[:, 0], rw3[:, t - 1], wkv_n)
    return x2.reshape(b, t, D_MODEL), new_state


def kernel(x_prompt, x_sample, cache_sb_k, cache_sb_v, cache_fox_k, cache_fox_v, cache_fox_logf, state_lru_conv, state_lru_h, state_rwkv_shift, state_rwkv_wkv, page_table, norm1_g, w_in, lru_conv_w, lru_conv_b, lru_wa, lru_ba, lru_wi, lru_bi, lru_lambda, fox_qnorm_g, fox_knorm_g, fox_fbias, rw_mu, rw_w0, rw_w2, rw_a0, rw_a2, rw_g2, rw_kk, rw_ka, rw_rk, rw_ln_g, rw_ln_b, w_branch, w_out, norm2_g, moe_wg, moe_bg, moe_we, moe_be, moe_w1, moe_w3, moe_w2):
    params = dict(norm1_g=norm1_g, w_in=w_in, lru_conv_w=lru_conv_w, lru_conv_b=lru_conv_b, lru_wa=lru_wa,
                  lru_ba=lru_ba, lru_wi=lru_wi, lru_bi=lru_bi, lru_lambda=lru_lambda, fox_qnorm_g=fox_qnorm_g,
                  fox_knorm_g=fox_knorm_g, fox_fbias=fox_fbias, rw_mu=rw_mu, rw_w0=rw_w0, rw_w2=rw_w2, rw_a0=rw_a0,
                  rw_a2=rw_a2, rw_g2=rw_g2, rw_kk=rw_kk, rw_ka=rw_ka, rw_rk=rw_rk, rw_ln_g=rw_ln_g, rw_ln_b=rw_ln_b,
                  w_branch=w_branch, w_out=w_out, norm2_g=norm2_g, moe_wg=moe_wg, moe_bg=moe_bg, moe_we=moe_we,
                  moe_be=moe_be, moe_w1=moe_w1, moe_w3=moe_w3, moe_w2=moe_w2)
    depth = w_in.shape[0]
    bp = x_prompt.shape[0]
    sb_kt, sb_vt = _pages_t(cache_sb_k), _pages_t(cache_sb_v)
    fox_kt, fox_vt = _pages_t(cache_fox_k), _pages_t(cache_fox_v)
    logf_pairs = _logf_pairs(cache_fox_logf)
    prompt_init = (jnp.zeros((bp, CONV_W - 1, BRANCH_W), F32), jnp.zeros((bp, BRANCH_W), F32),
                   jnp.zeros((bp, D_RW_IN), F32), jnp.zeros((bp, N_HEADS, HEAD_DIM, HEAD_DIM), F32))
    w_proj, w_gate = _regroup_w_in(w_in)
    y_p, y_s = x_prompt, x_sample
    st_p, st_s = [], []
    for l in range(depth):
        lw = _prep_layer(params, l, w_proj, w_gate)
        y_p, sp = _layer(y_p, lw, prompt_init, None)
        y_s, ss = _layer(y_s, lw, (state_lru_conv[l], state_lru_h[l], state_rwkv_shift[l], state_rwkv_wkv[l]),
                         (sb_kt, sb_vt, fox_kt, fox_vt, logf_pairs, page_table, l))
        st_p.append(sp)
        st_s.append(ss)
    stack = lambda sts: [jnp.stack([s[i] for s in sts], axis=0) for i in range(9)]
    return (y_p, y_s, *stack(st_p), *stack(st_s))
```

```python
import functools
import math

import jax
import jax.numpy as jnp
from jax import lax
from jax.experimental import pallas as pl
from jax.experimental.pallas import tpu as pltpu

F32 = jnp.float32
BF16 = jnp.bfloat16

D_MODEL = 1024
HEAD_DIM = 64
N_HEADS = 4
BRANCH_W = N_HEADS * HEAD_DIM
CONV_W = 4
LRU_C = 8.0
PAGE = 128
RW_TAIL = 128
D_RW_IN = 3 * BRANCH_W + RW_TAIL
N_MAIN = 8 * BRANCH_W
N_GROUPS = 4
N_EXPERTS = 16
D_EXP = 256
RMS_EPS = 1e-6
RW_GN_EPS = 64e-5
ATT_SCALE = HEAD_DIM ** -0.5
LANE = 128
RW_CHUNK = 64
VMEM_LIMIT = 56 * 1024 * 1024


def _cparams(sem):
    return pltpu.CompilerParams(dimension_semantics=sem, vmem_limit_bytes=VMEM_LIMIT)


def _dot(a, b):
    return jnp.dot(a.astype(BF16), b.astype(BF16), preferred_element_type=F32)


def _dot_nt(a, b):
    return lax.dot_general(a.astype(BF16), b.astype(BF16), (((1,), (1,)), ((), ())), preferred_element_type=F32)


def _dot_tn(a, b):
    return lax.dot_general(a.astype(BF16), b.astype(BF16), (((0,), (0,)), ((), ())), preferred_element_type=F32)


def _dot2(a, b_bf16):
    hi = a.astype(BF16)
    lo = (a - hi.astype(F32)).astype(BF16)
    return (jnp.dot(hi, b_bf16, preferred_element_type=F32) + jnp.dot(lo, b_bf16, preferred_element_type=F32))


def _softplus(x):
    return jnp.maximum(x, 0.0) + jnp.log1p(jnp.exp(-jnp.abs(x)))


def _sigmoid(x):
    return 1.0 / (1.0 + jnp.exp(-x))


def _gelu_tanh(x):
    return 0.5 * x * (1.0 + jnp.tanh(math.sqrt(2.0 / math.pi) * (x + 0.044715 * (x * x * x))))


def _rmsnorm(x, g):
    return x * lax.rsqrt(jnp.mean(x * x, axis=-1, keepdims=True) + RMS_EPS) * g


def _row_iota(shape):
    return lax.broadcasted_iota(jnp.int32, shape, 0)


def _col_iota(shape):
    return lax.broadcasted_iota(jnp.int32, shape, 1)


def _scan_rows(a, u):
    n = a.shape[0]
    row = _row_iota(a.shape)
    s = 1
    while s < n:
        valid = row >= s
        a_sh = pltpu.roll(a, s, 0)
        u_sh = pltpu.roll(u, s, 0)
        u = jnp.where(valid, a * u_sh + u, u)
        a = jnp.where(valid, a * a_sh, a)
        s *= 2
    return a, u


def _cumsum_rows(x):
    n = x.shape[0]
    row = _row_iota(x.shape)
    s = 1
    while s < n:
        x = x + jnp.where(row >= s, pltpu.roll(x, s, 0), 0.0)
        s *= 2
    return x


def _cumsum_lanes(x):
    ax = x.ndim - 1
    n = x.shape[ax]
    col = lax.broadcasted_iota(jnp.int32, x.shape, ax)
    s = 1
    while s < n:
        x = x + jnp.where(col >= s, pltpu.roll(x, s, ax), 0.0)
        s *= 2
    return x


def _head_ones():
    i = jnp.arange(BRANCH_W) // HEAD_DIM
    return (i[:, None] == i[None, :]).astype(BF16)


def _proj_kernel(x_ref, g_ref, w_ref, qg_ref, kg_ref, fb_ref, hs_ref,
                 lru_ref, rw_ref, sbq_ref, sbk_ref, sbv_ref, sbkb_ref, sbvb_ref,
                 fq_ref, fk_ref, fv_ref, fkb_ref, fvb_ref, logf_ref, *, state_t):
    xn = _rmsnorm(x_ref[...], g_ref[...]).astype(BF16)

    def seg(a, b):
        return jnp.dot(xn, w_ref[:, a:b], preferred_element_type=F32)

    def put_state(ref, val):
        if state_t:
            ref[0] = val.T
        else:
            ref[...] = val

    W = BRANCH_W
    lru_ref[...] = seg(0, 2 * W)
    sbq_ref[...] = (seg(2 * W, 3 * W) * ATT_SCALE).astype(BF16)
    k = seg(3 * W, 4 * W)
    put_state(sbk_ref, k)
    sbkb_ref[...] = k.astype(BF16)
    v = seg(4 * W, 5 * W)
    put_state(sbv_ref, v)
    sbvb_ref[...] = v.astype(BF16)
    hs = hs_ref[...]
    q = seg(5 * W, 6 * W)
    q = q * lax.rsqrt(_dot2(q * q, hs) * (1.0 / HEAD_DIM) + RMS_EPS) * qg_ref[...]
    fq_ref[...] = (q * ATT_SCALE).astype(BF16)
    k = seg(6 * W, 7 * W)
    k = k * lax.rsqrt(_dot2(k * k, hs) * (1.0 / HEAD_DIM) + RMS_EPS) * kg_ref[...]
    put_state(fk_ref, k)
    fkb_ref[...] = k.astype(BF16)
    v = seg(7 * W, 8 * W)
    put_state(fv_ref, v)
    ones = jnp.ones((v.shape[0], HEAD_DIM), F32)
    fvb_ref[...] = jnp.concatenate(
        [p for h in range(N_HEADS) for p in (v[:, h * HEAD_DIM:(h + 1) * HEAD_DIM], ones)], axis=1).astype(BF16)
    rw_ref[...] = seg(N_MAIN, N_MAIN + D_RW_IN)
    f = seg(N_MAIN + D_RW_IN, N_MAIN + D_RW_IN + LANE) + fb_ref[...]
    logf_ref[...] = -_softplus(-f)


def _proj(x2d, lw, tm, seq_len=None):
    n = x2d.shape[0]
    W = BRANCH_W
    row = lambda w: pl.BlockSpec((tm, w), lambda i: (i, 0))
    full = lambda a: pl.BlockSpec(a.shape, lambda i: (0,) * a.ndim)
    ins = (x2d, lw['norm1_g'], lw['w_proj'], lw['fox_qg'], lw['fox_kg'], lw['fox_fb'], lw['head_ones'])
    out_w = (2 * W, D_RW_IN, W, W, W, W, W, W, W, W, W, N_HEADS * LANE, LANE)
    out_dt = (F32, F32, BF16, F32, F32, BF16, BF16, BF16, F32, F32, BF16, BF16, F32)
    out_specs = [row(w) for w in out_w]
    out_shape = [jax.ShapeDtypeStruct((n, w), dt) for w, dt in zip(out_w, out_dt)]
    if seq_len is not None:
        per_seq = seq_len // tm
        for idx in (3, 4, 8, 9):
            out_specs[idx] = pl.BlockSpec((1, W, tm), lambda i: (i // per_seq, 0, i % per_seq))
            out_shape[idx] = jax.ShapeDtypeStruct((n // seq_len, W, seq_len), F32)
    layer = lw['layer']
    w_spec = pl.BlockSpec((None, D_MODEL, N_PROJ), lambda i: (layer, 0, 0))
    return pl.pallas_call(
        functools.partial(_proj_kernel, state_t=seq_len is not None),
        grid=(n // tm,),
        in_specs=[row(D_MODEL), full(ins[1]), w_spec] + [full(a) for a in ins[3:]],
        out_specs=out_specs,
        out_shape=out_shape,
        compiler_params=_cparams(("parallel",)),
        name="proj",
    )(*ins)


def _lru_kernel(xy_ref, conv0_ref, h0_ref, cw_ref, cb_ref, wa_ref, ba_ref, wi_ref, bi_ref, lam_ref,
                y_ref, convn_ref, hn_ref, tail_sc, h_sc, *, t_last):
    j = pl.program_id(1)
    W = BRANCH_W

    @pl.when(j == 0)
    def _():
        tail_sc[...] = conv0_ref[0]
        h_sc[...] = h0_ref[0]

    x = xy_ref[0, :, 0:W]
    y = xy_ref[0, :, W:2 * W]
    tt = x.shape[0]
    tail = tail_sc[...]
    row8 = _row_iota((8, W))
    cw = cw_ref[...]
    xc = cb_ref[...] + cw[CONV_W - 1:CONV_W, :] * x
    for kback in range(1, CONV_W):
        xs = pltpu.roll(x, kback, 0)
        head = jnp.where(row8 < kback, pltpu.roll(tail, kback, 0), xs[0:8])
        xs = head if tt == 8 else jnp.concatenate([head, xs[8:]], axis=0)
        xc = xc + cw[CONV_W - 1 - kback:CONV_W - kback, :] * xs
    r = _sigmoid(_dot(xc, wa_ref[...]) + ba_ref[...])
    ig = _sigmoid(_dot(xc, wi_ref[...]) + bi_ref[...])
    log_a = (-LRU_C) * r * _softplus(-lam_ref[...])
    a = jnp.exp(log_a)
    u = jnp.sqrt(-jnp.tanh(log_a) * (a * a + 1.0)) * (ig * xc)
    ap, hloc = _scan_rows(a, u)
    h = ap * h_sc[...] + hloc
    y_ref[0] = h * _gelu_tanh(y)
    h_sc[...] = h[tt - 1:tt, :]
    tail_sc[...] = x[tt - 8:tt, :]

    @pl.when(j == pl.num_programs(1) - 1)
    def _():
        hn_ref[0] = h[t_last - 1:t_last, :]
        convn_ref[0] = x[t_last - 8:t_last, :] if t_last >= 8 else jnp.where(
            row8 < 8 - t_last, pltpu.roll(tail, 8 - t_last, 0), pltpu.roll(x[0:8], 8 - t_last, 0))


def _lru(xy, conv0, h0, lw, tt, t_valid):
    b, tp, _ = xy.shape
    W = BRANCH_W
    nt = tp // tt
    t_last = t_valid - (nt - 1) * tt
    full = lambda a: pl.BlockSpec(a.shape, lambda i, j: (0,) * a.ndim)
    ws = (lw['lru_conv_w'], lw['lru_conv_b'], lw['lru_wa_bd'], lw['lru_ba'], lw['lru_wi_bd'], lw['lru_bi'], lw['lru_lambda'])
    return pl.pallas_call(
        functools.partial(_lru_kernel, t_last=t_last),
        grid=(b, nt),
        in_specs=[pl.BlockSpec((1, tt, 2 * W), lambda i, j: (i, j, 0)),
                  pl.BlockSpec((1, 8, W), lambda i, j: (i, 0, 0)),
                  pl.BlockSpec((1, 1, W), lambda i, j: (i, 0, 0))] + [full(a) for a in ws],
        out_specs=[pl.BlockSpec((1, tt, W), lambda i, j: (i, j, 0)),
                   pl.BlockSpec((1, 8, W), lambda i, j: (i, 0, 0)),
                   pl.BlockSpec((1, 1, W), lambda i, j: (i, 0, 0))],
        out_shape=[jax.ShapeDtypeStruct((b, tp, W), F32),
                   jax.ShapeDtypeStruct((b, 8, W), F32),
                   jax.ShapeDtypeStruct((b, 1, W), F32)],
        scratch_shapes=[pltpu.VMEM((8, W), F32), pltpu.VMEM((1, W), F32)],
        compiler_params=_cparams(("parallel", "arbitrary")),
        name="rglru",
    )(xy, conv0, h0, *ws)


def _block_diag(w):
    n, c, d = w.shape
    eye = jnp.eye(n, dtype=w.dtype)
    return (eye[:, None, :, None] * w[:, :, None, :]).reshape(n * c, n * d)


N_PROJ = N_MAIN + D_RW_IN + LANE
N_GATE = N_HEADS * D_MODEL


def _regroup_kernel(w_ref, wp_ref, wg_ref):
    o_f = N_MAIN
    o_rw = N_MAIN + N_HEADS
    o_g = o_rw + D_RW_IN
    rows = w_ref.shape[1]
    wp_ref[0, :, 0:o_f] = w_ref[0, :, 0:o_f].astype(BF16)
    wp_ref[0, :, o_f:o_f + D_RW_IN] = w_ref[0, :, o_rw:o_g].astype(BF16)
    f_cols = jnp.concatenate([w_ref[0, :, o_f:o_rw], jnp.zeros((rows, LANE - N_HEADS), F32)], axis=1)
    wp_ref[0, :, o_f + D_RW_IN:N_PROJ] = f_cols.astype(BF16)
    wg_ref[0] = w_ref[0, :, o_g:o_g + N_GATE].astype(BF16)


def _regroup_w_in(w_in):
    depth, d, n = w_in.shape
    rows = 256
    return pl.pallas_call(
        _regroup_kernel,
        grid=(depth, d // rows),
        in_specs=[pl.BlockSpec((1, rows, n), lambda l, i: (l, i, 0))],
        out_specs=[pl.BlockSpec((1, rows, N_PROJ), lambda l, i: (l, i, 0)),
                   pl.BlockSpec((1, rows, N_GATE), lambda l, i: (l, i, 0))],
        out_shape=[jax.ShapeDtypeStruct((depth, d, N_PROJ), BF16), jax.ShapeDtypeStruct((depth, d, N_GATE), BF16)],
        compiler_params=_cparams(("parallel", "parallel")),
        name="regroup_w_in",
    )(w_in)


def _prep_layer(p, l, w_proj, w_gate):
    row = lambda a: a[l].reshape(1, -1).astype(F32)
    lw = {
        'layer': l,
        'norm1_g': row(p['norm1_g']),
        'w_proj': w_proj,
        'w_gate': w_gate,
        'fox_qg': jnp.tile(p['fox_qnorm_g'][l], N_HEADS).reshape(1, -1),
        'fox_kg': jnp.tile(p['fox_knorm_g'][l], N_HEADS).reshape(1, -1),
        'fox_fb': jnp.pad(p['fox_fbias'][l], (0, LANE - N_HEADS)).reshape(1, -1),
        'head_ones': _head_ones(),
        'lru_conv_w': p['lru_conv_w'][l],
        'lru_conv_b': row(p['lru_conv_b']),
        'lru_wa_bd': _block_diag(p['lru_wa'][l]).astype(BF16),
        'lru_ba': row(p['lru_ba']),
        'lru_wi_bd': _block_diag(p['lru_wi'][l]).astype(BF16),
        'lru_bi': row(p['lru_bi']),
        'lru_lambda': row(p['lru_lambda']),
        'rw_mu': row(p['rw_mu']),
        'rw_w0': row(p['rw_w0']),
        'rw_a0': row(p['rw_a0']),
        'rw_w2p': jnp.pad(p['rw_w2'][l], ((0, 96), (0, 0))).astype(BF16),
        'rw_a2p': jnp.pad(p['rw_a2'][l], ((32, 64), (0, 0))).astype(BF16),
        'rw_g2p': jnp.pad(p['rw_g2'][l], ((64, 0), (0, 0))).astype(BF16),
        'rw_kk': row(p['rw_kk']),
        'rw_ka': row(p['rw_ka']),
        'rw_rk': row(p['rw_rk']),
        'rw_ln_g': row(p['rw_ln_g']),
        'rw_ln_b': row(p['rw_ln_b']),
        'w_branch': p['w_branch'][l].astype(BF16),
        'w_out': p['w_out'][l].astype(BF16),
        'norm2_g': row(p['norm2_g']),
        'moe_w1': p['moe_w1'][l].astype(BF16),
        'moe_w3': p['moe_w3'][l].astype(BF16),
        'moe_w2': p['moe_w2'][l].astype(BF16),
    }
    w_r = jnp.concatenate([p['moe_we'][l], p['moe_wg'][l]], axis=1)
    lw['moe_wr'] = jnp.pad(w_r, ((0, 0), (0, LANE - w_r.shape[1])))
    b_r = jnp.concatenate([p['moe_be'][l], p['moe_bg'][l]])
    lw['moe_br'] = jnp.pad(b_r, (0, LANE - b_r.shape[0])).reshape(1, -1)
    return lw


def _rwkv_kernel(pd_ref, shift0_ref, st0_ref, mu_ref, w0_ref, w2_ref, a0_ref, a2_ref, g2_ref,
                 kkw_ref, ka_ref, rk_ref, lng_ref, lnb_ref, hs_ref,
                 y_ref, stn_ref, prev_sc, st_sc, *, t_valid):
    j = pl.program_id(1)
    W = BRANCH_W
    nb, C = pd_ref.shape[0], pd_ref.shape[1]
    R4 = N_HEADS * C

    @pl.when(j == 0)
    def _():
        prev_sc[...] = shift0_ref[...]
        st_sc[...] = st0_ref[...]

    hs = hs_ref[...]
    row = _row_iota((C, 1))
    live = (row + j * C) < t_valid
    rr = _row_iota((R4, R4))
    cc = _col_iota((R4, R4))
    same_head = (rr // C) == (cc // C)
    strict = same_head & ((cc % C) < (rr % C))
    incl = same_head & ((cc % C) <= (rr % C))
    eye = rr == cc
    tile4 = lambda x: jnp.concatenate([x] * N_HEADS, axis=0)
    stack = lambda x: jnp.concatenate([x[:, h * HEAD_DIM:(h + 1) * HEAD_DIM] for h in range(N_HEADS)], axis=0)

    def chain(bi):
        pd = pd_ref[bi]
        prv = jnp.where(row == 0, prev_sc[bi], pltpu.roll(pd, 1, 0))
        ps = pd + (prv - pd) * mu_ref[...]
        r = ps[:, 0:W]
        k = ps[:, W:2 * W]
        v = ps[:, 2 * W:3 * W]
        tail = ps[:, 3 * W:3 * W + RW_TAIL]
        w_log = -_softplus(-(w0_ref[...] + _dot(jnp.tanh(tail), w2_ref[...]))) - 0.5
        logw = -jnp.exp(w_log)
        a = _sigmoid(a0_ref[...] + _dot(tail, a2_ref[...]))
        g = _dot(_sigmoid(tail), g2_ref[...])
        kk = k * kkw_ref[...]
        kk_ss = _dot2(kk * kk, hs)
        yield
        kk = kk / jnp.maximum(jnp.sqrt(kk_ss), 1e-12)
        k = k * (1.0 + (a - 1.0) * ka_ref[...])
        logw = jnp.where(live, logw, 0.0)
        kk = jnp.where(live, kk, 0.0)
        kl = jnp.where(live, k, 0.0)

        c = _cumsum_rows(logw)
        e_pos = jnp.exp(c)
        e_neg = jnp.exp(-c)
        e_end = e_pos[C - 1:C, :]
        a_bd = jnp.where(same_head, tile4(-kk * jnp.exp(c - logw)), 0.0)
        r_bd = jnp.where(same_head, tile4(r * e_pos), 0.0)
        b_t = kk * a * e_neg
        k_t = kl * e_neg
        ar = jnp.concatenate([a_bd, r_bd], axis=0).astype(BF16)
        bk = jnp.concatenate([tile4(b_t), tile4(k_t)], axis=0).astype(BF16)
        m = _dot_nt(ar, bk)
        st = st_sc[bi]
        ars = _dot(ar, st)
        v_st = stack(v)
        yield
        L = jnp.where(strict, m[0:R4, 0:R4], 0.0)
        X = ars[0:R4] + _dot(jnp.where(strict, m[0:R4, R4:2 * R4], 0.0), v_st)
        yield
        s = 1
        while s < C:
            s *= 2
            if s < C:
                lx = _dot(L, jnp.concatenate([L, X], axis=1))
                L, X = lx[:, 0:R4], X + lx[:, R4:R4 + HEAD_DIM]
            else:
                X = X + _dot(L, X)
            yield
        y_st = (ars[R4:2 * R4] + _dot(jnp.where(incl, m[R4:2 * R4, 0:R4], 0.0), X)
                + _dot(jnp.where(incl, m[R4:2 * R4, R4:2 * R4], 0.0), v_st))
        bg_bd = jnp.where(same_head, tile4(b_t * e_end), 0.0)
        kg_bd = jnp.where(same_head, tile4(k_t * e_end), 0.0)
        e_col = jnp.sum(jnp.where(eye, jnp.broadcast_to(e_end, (R4, R4)), 0.0), axis=1, keepdims=True)
        st_sc[bi] = st * e_col + _dot_tn(bg_bd, X) + _dot_tn(kg_bd, v_st)
        yield

        y = jnp.concatenate([y_st[h * C:(h + 1) * C] for h in range(N_HEADS)], axis=1)
        inv_d = 1.0 / HEAD_DIM
        mean = _dot2(y, hs) * inv_d
        dv = y - mean
        var = _dot2(dv * dv, hs) * inv_d
        yn = dv * lax.rsqrt(var + RW_GN_EPS) * lng_ref[...] + lnb_ref[...]
        bonus = _dot2(r * k * rk_ref[...], hs) * v
        y_ref[bi] = (yn + bonus) * g
        prev_sc[bi] = pd[C - 1:C, :]

    chains = [chain(bi) for bi in range(nb)]
    while chains:
        chains = [ch for ch in chains if next(ch, chains) is not chains]

    @pl.when(j == pl.num_programs(1) - 1)
    def _():
        stn_ref[...] = st_sc[...]


def _rwkv(pd, shift0, s0, lw, t_valid, nb):
    b, tp, _ = pd.shape
    W = BRANCH_W
    C = RW_CHUNK
    st0 = jnp.swapaxes(s0, 2, 3).reshape(b, W, HEAD_DIM)
    full = lambda a: pl.BlockSpec(a.shape, lambda i, j: (0,) * a.ndim)
    ws = (lw['rw_mu'], lw['rw_w0'], lw['rw_w2p'], lw['rw_a0'], lw['rw_a2p'], lw['rw_g2p'],
          lw['rw_kk'], lw['rw_ka'], lw['rw_rk'], lw['rw_ln_g'], lw['rw_ln_b'], lw['head_ones'])
    y, stn = pl.pallas_call(
        functools.partial(_rwkv_kernel, t_valid=t_valid),
        grid=(b // nb, tp // C),
        in_specs=[pl.BlockSpec((nb, C, D_RW_IN), lambda i, j: (i, j, 0)),
                  pl.BlockSpec((nb, 1, D_RW_IN), lambda i, j: (i, 0, 0)),
                  pl.BlockSpec((nb, W, HEAD_DIM), lambda i, j: (i, 0, 0))] + [full(a) for a in ws],
        out_specs=[pl.BlockSpec((nb, C, W), lambda i, j: (i, j, 0)),
                   pl.BlockSpec((nb, W, HEAD_DIM), lambda i, j: (i, 0, 0))],
        out_shape=[jax.ShapeDtypeStruct((b, tp, W), F32),
                   jax.ShapeDtypeStruct((b, W, HEAD_DIM), F32)],
        scratch_shapes=[pltpu.VMEM((nb, 1, D_RW_IN), F32), pltpu.VMEM((nb, W, HEAD_DIM), F32)],
        compiler_params=_cparams(("parallel", "arbitrary")),
        name="rwkv7",
    )(pd, shift0, st0, *ws)
    return y, jnp.swapaxes(stn.reshape(b, N_HEADS, HEAD_DIM, HEAD_DIM), 2, 3)


ATT_BLK = 256
FOX_KEY_BLK = 512


def _suffix_ones(n):
    i = jnp.arange(n)
    return jnp.concatenate([(i[:, None] > i[None, :]), jnp.ones((n, n), bool)], axis=1).astype(BF16)


def _sb_prompt_kernel(q_ref, k_ref, v_ref, m_ref, o_ref, run_sc, acc_sc):
    i = pl.program_id(1)
    T = ATT_BLK
    msuf = m_ref[...]
    lower = _col_iota((T, T)) < _row_iota((T, T))
    heads = [slice(h * HEAD_DIM, (h + 1) * HEAD_DIM) for h in range(N_HEADS)]
    run_sc[...] = jnp.zeros_like(run_sc)
    acc_sc[...] = jnp.zeros_like(acc_sc)

    def block(kj, diag):
        start = pl.multiple_of(kj * T, T)
        zs = [_dot_nt(q_ref[0, :, sl], k_ref[0, pl.ds(start, T), sl]) for sl in heads]
        log_betas = [jnp.minimum(z, 0.0) - jnp.log(1.0 + jnp.exp(-jnp.abs(z))) for z in zs]
        lss = [lb - z for lb, z in zip(log_betas, zs)]
        if diag:
            lss = [jnp.where(lower, ls, 0.0) for ls in lss]
        sts = [_dot(ls, msuf) for ls in lss]
        ws = [jnp.exp(lb + st + run_sc[h]) for h, (lb, st) in enumerate(zip(log_betas, sts))]
        if diag:
            ws = [jnp.where(lower, w, 0.0) for w in ws]
        for h, sl in enumerate(heads):
            acc_sc[h] += _dot(ws[h], v_ref[0, pl.ds(start, T), sl])
            run_sc[h] += sts[h][:, 0:1] + lss[h][:, 0:1]

    block(i, True)

    def body(jj, carry):
        block(i - 1 - jj, False)
        return carry

    lax.fori_loop(0, i, body, 0)
    o_ref[0] = jnp.concatenate([acc_sc[h] for h in range(N_HEADS)], axis=1)


def _sb_prompt(q, k, v):
    b, t, W = q.shape
    msuf = _suffix_ones(ATT_BLK)[:, :ATT_BLK]
    return pl.pallas_call(
        _sb_prompt_kernel,
        grid=(b, t // ATT_BLK),
        in_specs=[pl.BlockSpec((1, ATT_BLK, W), lambda i, j: (i, j, 0)),
                  pl.BlockSpec((1, t, W), lambda i, j: (i, 0, 0)),
                  pl.BlockSpec((1, t, W), lambda i, j: (i, 0, 0)),
                  pl.BlockSpec(msuf.shape, lambda i, j: (0, 0))],
        out_specs=pl.BlockSpec((1, ATT_BLK, W), lambda i, j: (i, j, 0)),
        out_shape=jax.ShapeDtypeStruct((b, t, W), F32),
        scratch_shapes=[pltpu.VMEM((N_HEADS, ATT_BLK, 1), F32), pltpu.VMEM((N_HEADS, ATT_BLK, HEAD_DIM), F32)],
        compiler_params=_cparams(("parallel", "arbitrary")),
        name="sb_prompt",
    )(q, k, v, msuf)


def _fox_prompt_kernel(q_ref, k_ref, v_ref, fq_ref, fk_ref, o_ref, m_sc, acc_sc):
    i = pl.program_id(1)
    T = ATT_BLK
    TK = FOX_KEY_BLK
    last = (i * T) // TK
    heads = [slice(h * HEAD_DIM, (h + 1) * HEAD_DIM) for h in range(N_HEADS)]
    m_sc[...] = jnp.full_like(m_sc, -1e30)
    acc_sc[...] = jnp.zeros_like(acc_sc)

    def block(kj, diag):
        start = pl.multiple_of(kj * TK, TK)
        keys = pl.ds(start, TK)
        ss = [_dot_nt(q_ref[0, :, sl], k_ref[0, keys, sl]) + (fq_ref[0, :, h:h + 1] - fk_ref[0, h:h + 1, keys])
              for h, sl in enumerate(heads)]
        if diag:
            visible = (_col_iota((T, TK)) + start) <= (_row_iota((T, TK)) + i * T)
            ss = [jnp.where(visible, s, -jnp.inf) for s in ss]
        m_olds = [m_sc[h] for h in range(N_HEADS)]
        m_news = [jnp.maximum(m, jnp.max(s, axis=1, keepdims=True)) for m, s in zip(m_olds, ss)]
        ps = [jnp.exp(s - m) for s, m in zip(ss, m_news)]
        for h in range(N_HEADS):
            alpha = jnp.exp(m_olds[h] - m_news[h])
            acc_sc[h] = acc_sc[h] * alpha + _dot(ps[h], v_ref[0, keys, h * LANE:(h + 1) * LANE])
            m_sc[h] = m_news[h]

    block(last, True)

    def body(jj, carry):
        block(last - 1 - jj, False)
        return carry

    lax.fori_loop(0, last, body, 0)
    o_ref[0] = jnp.concatenate([acc_sc[h][:, 0:HEAD_DIM] / acc_sc[h][:, HEAD_DIM:LANE] for h in range(N_HEADS)],
                               axis=1)


def _fox_prompt(q, k, v_ones, f_col, f_row):
    b, t, W = q.shape
    return pl.pallas_call(
        _fox_prompt_kernel,
        grid=(b, t // ATT_BLK),
        in_specs=[pl.BlockSpec((1, ATT_BLK, W), lambda i, j: (i, j, 0)),
                  pl.BlockSpec((1, t, W), lambda i, j: (i, 0, 0)),
                  pl.BlockSpec((1, t, N_HEADS * LANE), lambda i, j: (i, 0, 0)),
                  pl.BlockSpec((1, ATT_BLK, 8), lambda i, j: (i, j, 0)),
                  pl.BlockSpec((1, 8, t), lambda i, j: (i, 0, 0))],
        out_specs=pl.BlockSpec((1, ATT_BLK, W), lambda i, j: (i, j, 0)),
        out_shape=jax.ShapeDtypeStruct((b, t, W), F32),
        scratch_shapes=[pltpu.VMEM((N_HEADS, ATT_BLK, 1), F32), pltpu.VMEM((N_HEADS, ATT_BLK, LANE), F32)],
        compiler_params=_cparams(("parallel", "arbitrary")),
        name="fox_prompt",
    )(q, k, v_ones, f_col, f_row)


def _cumsum_kernel(x_ref, o_ref):
    n = x_ref.shape[2] // LANE
    parts = [_cumsum_lanes(x_ref[0, :, c * LANE:(c + 1) * LANE]) for c in range(n)]
    carry = jnp.zeros((8, LANE), F32)
    for c, s in enumerate(parts):
        o_ref[0, :, c * LANE:(c + 1) * LANE] = s + carry
        carry = carry + jnp.broadcast_to(s[:, LANE - 1:LANE], s.shape)


def _cumsum_prompt(logf_row):
    b, _, t = logf_row.shape
    return pl.pallas_call(
        _cumsum_kernel,
        grid=(b,),
        in_specs=[pl.BlockSpec((1, 8, t), lambda i: (i, 0, 0))],
        out_specs=pl.BlockSpec((1, 8, t), lambda i: (i, 0, 0)),
        out_shape=jax.ShapeDtypeStruct((b, 8, t), F32),
        compiler_params=_cparams(("parallel",)),
        name="logf_cumsum",
    )(logf_row)


PAGES_PER_STEP = 8
PAGE_SLOTS = 3
QROWS = N_HEADS * 8


def _f_sample_kernel(pt_ref, pool_ref, new_ref, o_ref, x_sc):
    i = pl.program_id(0)
    n_pages = pt_ref.shape[1]
    n = n_pages + 1

    def gather(j, carry):
        p = pt_ref[i, j]
        tile = pool_ref[0, p // 2]
        x_sc[j] = jnp.where(p % 2 == 1, pltpu.roll(tile, N_HEADS, 0), tile)
        return carry

    lax.fori_loop(0, n_pages, gather, 0)
    x_sc[n_pages] = new_ref[0]
    c = _cumsum_lanes(x_sc[...])
    tot = jnp.broadcast_to(c[:, :, LANE - 1:LANE], c.shape)
    inc = tot
    s = 1
    while s < n:
        inc = inc + jnp.concatenate([jnp.zeros((s, 8, LANE), F32), inc[0:n - s]], axis=0)
        s *= 2
    o_ref[0] = c + (inc - tot)


def _f_sample(pool_pairs, logf_new, page_table, layer):
    b, n_pages = page_table.shape
    return pl.pallas_call(
        _f_sample_kernel,
        grid_spec=pltpu.PrefetchScalarGridSpec(
            num_scalar_prefetch=1,
            grid=(b,),
            in_specs=[pl.BlockSpec((1,) + pool_pairs.shape[1:], lambda i, pt: (layer, 0, 0, 0)),
                      pl.BlockSpec((1, 8, LANE), lambda i, pt: (i, 0, 0))],
            out_specs=pl.BlockSpec((1, n_pages + 1, 8, LANE), lambda i, pt: (i, 0, 0, 0)),
            scratch_shapes=[pltpu.VMEM((n_pages + 1, 8, LANE), F32)]),
        out_shape=jax.ShapeDtypeStruct((b, n_pages + 1, 8, LANE), F32),
        compiler_params=_cparams(("arbitrary",)),
        name="logf_cumsum_paged",
    )(page_table, pool_pairs, logf_new)


def _pages_t(cache):
    l, n, p, h, d = cache.shape
    return jnp.transpose(cache, (0, 1, 3, 4, 2)).reshape(l, n, h * d, p)


def _logf_pairs(cache_logf):
    l, n, p, h = cache_logf.shape
    return jnp.swapaxes(cache_logf, 2, 3).reshape(l, n // 2, 2 * h, p)


def _rows_t(x):
    return jnp.pad(jnp.swapaxes(x, 1, 2), ((0, 0), (0, 0), (0, LANE - x.shape[1])))


def _head_mask():
    return (jnp.arange(QROWS)[:, None] // 8) == (jnp.arange(BRANCH_W)[None, :] // HEAD_DIM)


def _heads_q(q):
    b, t, _ = q.shape
    q8 = jnp.pad(q, ((0, 0), (0, 8 - t), (0, 0)))
    return jnp.where(_head_mask(), jnp.tile(q8, (1, N_HEADS, 1)), 0).astype(BF16)


def _heads_out(o, t):
    b = o.shape[0]
    o4 = jnp.where(_head_mask(), o, 0.0).reshape(b, N_HEADS, 8, BRANCH_W)
    return jnp.sum(o4, axis=1)[:, :t, :]


def _sample_attn_kernel(pt_ref, qs_ref, qf_ref, fq_ref, ksn_ref, vsn_ref, kfn_ref, vfn_ref, f_ref, m_ref,
                        ck_sb, cv_sb, ck_fx, cv_fx, os_ref, of_ref,
                        kb_sb, vb_sb, kb_fx, vb_fx, sems, run_sc, accs_sc, m_sc, l_sc, accf_sc, *, layer, n_new):
    G = PAGES_PER_STEP
    b = pl.program_id(0)
    n_pages = pt_ref.shape[1]
    n_groups = n_pages // G
    caches = (ck_sb, cv_sb, ck_fx, cv_fx)
    bufs = (kb_sb, vb_sb, kb_fx, vb_fx)
    qs = qs_ref[0]
    qf = qf_ref[0]
    fq = fq_ref[0]
    msuf = m_ref[...]

    def page_of(grp, i):
        return n_pages - 1 - (grp * G + i)

    def copies(grp, slot):
        return [pltpu.make_async_copy(caches[a].at[layer, pt_ref[b, page_of(grp, i)]], bufs[a].at[slot, i],
                                      sems.at[slot, a])
                for a in range(len(caches)) for i in range(G)]

    def sb_blocks(kts, vts, mask):
        zs = [jnp.dot(qs, kt.astype(BF16), preferred_element_type=F32) for kt in kts]
        ls = [-_softplus(z) for z in zs]
        if mask is not None:
            ls = [jnp.where(mask, l, 0.0) for l in ls]
        st = _dot2(jnp.concatenate(ls, axis=0) if len(ls) > 1 else ls[0], msuf)
        run = run_sc[...]
        acc = accs_sc[...]
        for g in range(len(kts)):
            sg = st[g * QROWS:(g + 1) * QROWS]
            w = jnp.exp(zs[g] + ls[g] + sg[:, 0:LANE] + run)
            if mask is not None:
                w = jnp.where(mask, w, 0.0)
            acc = acc + _dot_nt(w, vts[g])
            run = run + sg[:, LANE:2 * LANE]
        run_sc[...] = run
        accs_sc[...] = acc

    def head_rows(f):
        return jnp.concatenate([jnp.broadcast_to(f[h:h + 1, :], (8, LANE)) for h in range(N_HEADS)], axis=0)

    def fox_blocks(kts, vts, fks, mask):
        ss = [jnp.dot(qf, kt.astype(BF16), preferred_element_type=F32) + (fq - head_rows(fk))
              for kt, fk in zip(kts, fks)]
        if mask is not None:
            ss = [jnp.where(mask, s, -jnp.inf) for s in ss]
        s_max = ss[0]
        for s in ss[1:]:
            s_max = jnp.maximum(s_max, s)
        m_old = m_sc[...]
        m_new = jnp.maximum(m_old, jnp.max(s_max, axis=1, keepdims=True))
        alpha = jnp.exp(m_old - m_new)
        ps = [jnp.exp(s - m_new) for s in ss]
        p_sum = ps[0]
        pv = _dot_nt(ps[0], vts[0])
        for p, vt in zip(ps[1:], vts[1:]):
            p_sum = p_sum + p
            pv = pv + _dot_nt(p, vt)
        l_sc[...] = l_sc[...] * alpha + jnp.sum(p_sum, axis=1, keepdims=True)
        accf_sc[...] = accf_sc[...] * alpha + pv
        m_sc[...] = m_new

    for grp in range(PAGE_SLOTS - 1):
        for c in copies(grp, grp):
            c.start()

    run_sc[...] = jnp.zeros_like(run_sc)
    accs_sc[...] = jnp.zeros_like(accs_sc)
    m_sc[...] = jnp.full_like(m_sc, -1e30)
    l_sc[...] = jnp.zeros_like(l_sc)
    accf_sc[...] = jnp.zeros_like(accf_sc)
    t = _row_iota((QROWS, LANE)) % 8
    s = _col_iota((QROWS, LANE))
    sb_blocks([ksn_ref[0]], [vsn_ref[0]], (s < t) & (s < n_new))
    fox_blocks([kfn_ref[0]], [vfn_ref[0]], [f_ref[0, n_pages]], (s <= t) & (s < n_new))

    def group(grp, carry):
        slot = grp % PAGE_SLOTS
        ahead = grp + (PAGE_SLOTS - 1)

        @pl.when(ahead < n_groups)
        def _():
            for c in copies(ahead, ahead % PAGE_SLOTS):
                c.start()

        for c in copies(grp, slot):
            c.wait()
        sb_blocks([kb_sb[slot, i] for i in range(G)], [vb_sb[slot, i] for i in range(G)], None)
        fox_blocks([kb_fx[slot, i] for i in range(G)], [vb_fx[slot, i] for i in range(G)],
                   [f_ref[0, page_of(grp, i)] for i in range(G)], None)
        return carry

    lax.fori_loop(0, n_groups, group, 0)
    os_ref[0] = accs_sc[...]
    of_ref[0] = accf_sc[...] / l_sc[...]


def _sample_attn(q_sb, k_sb, v_sb, q_fx, k_fx, v_fx, logf_new, caches, pool_pairs, page_table, layer):
    b, t, _ = q_sb.shape
    n_pages = page_table.shape[1]
    G = PAGES_PER_STEP
    lf = jnp.pad(jnp.swapaxes(logf_new, 1, 2), ((0, 0), (0, 8 - N_HEADS), (0, LANE - t)))
    f_all = _f_sample(pool_pairs, lf, page_table, layer)
    f_q = f_all[:, n_pages, :N_HEADS, :8]
    f_q = jnp.broadcast_to(f_q.reshape(b, QROWS, 1), (b, QROWS, LANE))
    msuf = _suffix_ones(LANE)
    q_spec = pl.BlockSpec((1, QROWS, BRANCH_W), lambda i, pt: (i, 0, 0))
    new_spec = pl.BlockSpec((1, BRANCH_W, LANE), lambda i, pt: (i, 0, 0))
    any_spec = pl.BlockSpec(memory_space=pl.ANY)
    page_buf = pltpu.VMEM((PAGE_SLOTS, G, BRANCH_W, PAGE), F32)
    o_sb, o_fx = pl.pallas_call(
        functools.partial(_sample_attn_kernel, layer=layer, n_new=t),
        grid_spec=pltpu.PrefetchScalarGridSpec(
            num_scalar_prefetch=1,
            grid=(b,),
            in_specs=[q_spec, q_spec, pl.BlockSpec((1, QROWS, LANE), lambda i, pt: (i, 0, 0)),
                      new_spec, new_spec, new_spec, new_spec,
                      pl.BlockSpec((1, n_pages + 1, 8, LANE), lambda i, pt: (i, 0, 0, 0)),
                      pl.BlockSpec(msuf.shape, lambda i, pt: (0, 0)),
                      any_spec, any_spec, any_spec, any_spec],
            out_specs=[q_spec, q_spec],
            scratch_shapes=[page_buf, page_buf, page_buf, page_buf, pltpu.SemaphoreType.DMA((PAGE_SLOTS, 4)),
                            pltpu.VMEM((QROWS, LANE), F32), pltpu.VMEM((QROWS, BRANCH_W), F32),
                            pltpu.VMEM((QROWS, 1), F32), pltpu.VMEM((QROWS, 1), F32),
                            pltpu.VMEM((QROWS, BRANCH_W), F32)]),
        out_shape=[jax.ShapeDtypeStruct((b, QROWS, BRANCH_W), F32)] * 2,
        compiler_params=_cparams(("arbitrary",)),
        name="sample_attn",
    )(page_table, _heads_q(q_sb), _heads_q(q_fx), f_q, _rows_t(k_sb), _rows_t(v_sb), _rows_t(k_fx), _rows_t(v_fx),
      f_all, msuf, *caches)
    return _heads_out(o_sb, t), _heads_out(o_fx, t)


def _merge_kernel(x_ref, ya_ref, yb_ref, yc_ref, yd_ref, g_ref, wg_ref, wb_ref, wo_ref, o_ref):
    x = x_ref[...]
    xn = _rmsnorm(x, g_ref[...]).astype(BF16)
    m = None
    for n, y_ref in enumerate((ya_ref, yb_ref, yc_ref, yd_ref)):
        gate = _sigmoid(jnp.dot(xn, wg_ref[:, n * D_MODEL:(n + 1) * D_MODEL], preferred_element_type=F32))
        term = gate * _dot(y_ref[...], wb_ref[n])
        m = term if m is None else m + term
    o_ref[...] = x + _dot(m, wo_ref[...])


def _merge(x2d, ys, lw, tm):
    n = x2d.shape[0]
    row = lambda w: pl.BlockSpec((tm, w), lambda i: (i, 0))
    full = lambda a: pl.BlockSpec(a.shape, lambda i: (0,) * a.ndim)
    ws = (lw['norm1_g'], lw['w_gate'], lw['w_branch'], lw['w_out'])
    layer = lw['layer']
    wg_spec = pl.BlockSpec((None, D_MODEL, N_GATE), lambda i: (layer, 0, 0))
    return pl.pallas_call(
        _merge_kernel,
        grid=(n // tm,),
        in_specs=[row(D_MODEL)] + [row(BRANCH_W)] * 4 + [full(ws[0]), wg_spec, full(ws[2]), full(ws[3])],
        out_specs=row(D_MODEL),
        out_shape=jax.ShapeDtypeStruct((n, D_MODEL), F32),
        compiler_params=_cparams(("parallel",)),
        name="merge",
    )(x2d, *ys, *ws)


def _router_gate(logits):
    lane = _col_iota(logits.shape)
    lane_f = lane.astype(F32)
    big = float(LANE)
    neg = -jnp.inf
    gl = jnp.where((lane >= N_EXPERTS) & (lane < N_EXPERTS + N_GROUPS), logits, neg)
    g_max = jnp.max(gl, axis=1, keepdims=True)
    g_idx = jnp.min(jnp.where(gl == g_max, lane_f, big), axis=1, keepdims=True) - float(N_EXPERTS)
    g_w = 1.0 / jnp.sum(jnp.exp(gl - g_max), axis=1, keepdims=True)
    per_group = N_EXPERTS // N_GROUPS
    lo = g_idx * float(per_group)
    el = jnp.where((lane_f >= lo) & (lane_f < lo + float(per_group)), logits, neg)
    m1 = jnp.max(el, axis=1, keepdims=True)
    i1 = jnp.min(jnp.where(el == m1, lane_f, big), axis=1, keepdims=True)
    el2 = jnp.where(lane_f == i1, neg, el)
    m2 = jnp.max(el2, axis=1, keepdims=True)
    i2 = jnp.min(jnp.where(el2 == m2, lane_f, big), axis=1, keepdims=True)
    d = jnp.exp(m2 - m1)
    w1 = g_w / (1.0 + d)
    return jnp.where(lane_f == i1, w1, 0.0) + jnp.where(lane_f == i2, w1 * d, 0.0)


def _moe_kernel(x_ref, g_ref, wr_ref, br_ref, w1_ref, w3_ref, w2_ref, o_ref, xn_sc, gate_sc, acc_sc):
    e = pl.program_id(1)

    @pl.when(e == 0)
    def _():
        x = x_ref[...]
        xn = _rmsnorm(x, g_ref[...])
        xn_sc[...] = xn.astype(BF16)
        logits = jnp.dot(xn, wr_ref[...], preferred_element_type=F32, precision=lax.Precision.HIGHEST) + br_ref[...]
        gate_sc[...] = _router_gate(logits)
        acc_sc[...] = x

    xn = xn_sc[...]
    a = jnp.dot(xn, w1_ref[0], preferred_element_type=F32)
    h = a * _sigmoid(a) * jnp.dot(xn, w3_ref[0], preferred_element_type=F32)
    ge = jnp.sum(jnp.where(_col_iota(gate_sc.shape) == e, gate_sc[...], 0.0), axis=1, keepdims=True)
    acc_sc[...] += _dot(h * ge, w2_ref[0])

    @pl.when(e == pl.num_programs(1) - 1)
    def _():
        o_ref[...] = acc_sc[...]


def _moe(x2d, lw, tm):
    n = x2d.shape[0]
    row = pl.BlockSpec((tm, D_MODEL), lambda i, e: (i, 0))
    full = lambda a: pl.BlockSpec(a.shape, lambda i, e: (0,) * a.ndim)
    return pl.pallas_call(
        _moe_kernel,
        grid=(n // tm, N_EXPERTS),
        in_specs=[row, full(lw['norm2_g']), full(lw['moe_wr']), full(lw['moe_br']),
                  pl.BlockSpec((1, D_MODEL, D_EXP), lambda i, e: (e, 0, 0)),
                  pl.BlockSpec((1, D_MODEL, D_EXP), lambda i, e: (e, 0, 0)),
                  pl.BlockSpec((1, D_EXP, D_MODEL), lambda i, e: (e, 0, 0))],
        out_specs=row,
        out_shape=jax.ShapeDtypeStruct((n, D_MODEL), F32),
        scratch_shapes=[pltpu.VMEM((tm, D_MODEL), BF16), pltpu.VMEM((tm, LANE), F32), pltpu.VMEM((tm, D_MODEL), F32)],
        compiler_params=_cparams(("parallel", "arbitrary")),
        name="moe",
    )(x2d, lw['norm2_g'], lw['moe_wr'], lw['moe_br'], lw['moe_w1'], lw['moe_w3'], lw['moe_w2'])


def _pad_rows(a, mult):
    t = a.shape[1]
    tp = -(-t // mult) * mult
    return a if tp == t else jnp.pad(a, ((0, 0), (0, tp - t), (0, 0)))


def _layer(x, lw, state, paged):
    b, t, _ = x.shape
    n = b * t
    W = BRANCH_W
    conv0, h0, shift0, wkv0 = state
    tm = min(n, 256)
    x2d = x.reshape(n, D_MODEL)
    state_t = paged is None and t % tm == 0
    (lru_xy, rw_in, sbq, sbk, sbv, sbkb, sbvb, fq, fk, fv, fkb, fvb, logf_pad) = _proj(
        x2d, lw, tm, t if state_t else None)
    r3 = lambda a: a.reshape(b, t, a.shape[-1])
    logf = r3(logf_pad[:, :N_HEADS])

    tt = 256 if t % 256 == 0 else 8
    y_a, conv_n, h_n = _lru(_pad_rows(r3(lru_xy), tt), jnp.pad(conv0, ((0, 0), (8 - (CONV_W - 1), 0), (0, 0))),
                            h0[:, None, :], lw, tt, t)
    rw3 = r3(rw_in)
    y_d, wkv_n = _rwkv(_pad_rows(rw3, RW_CHUNK), shift0[:, None, :], wkv0, lw, t, 4)

    if paged is None:
        y_b = _sb_prompt(r3(sbq), r3(sbkb), r3(sbvb))
        f_row = _cumsum_prompt(jnp.pad(jnp.swapaxes(logf, 1, 2), ((0, 0), (0, 8 - N_HEADS), (0, 0))))
        y_c = _fox_prompt(r3(fq), r3(fkb), r3(fvb), jnp.swapaxes(f_row, 1, 2), f_row)
    else:
        sb_kt, sb_vt, fox_kt, fox_vt, logf_pairs, page_table, layer = paged
        y_b, y_c = _sample_attn(r3(sbq), r3(sbk), r3(sbv), r3(fq), r3(fk), r3(fv), logf,
                                (sb_kt, sb_vt, fox_kt, fox_vt), logf_pairs, page_table, layer)

    ys = (y_a[:, :t].reshape(n, W), y_b.reshape(n, W), y_c.reshape(n, W), y_d[:, :t].reshape(n, W))
    x1 = _merge(x2d, ys, lw, tm)
    x2 = _moe(x1, lw, min(n, 1024))
    if state_t:
        hd = lambda a: jnp.transpose(a.reshape(b, N_HEADS, HEAD_DIM, t), (0, 3, 1, 2))
    else:
        hd = lambda a: a.reshape(b, t, N_HEADS, HEAD_DIM)
    new_state = (hd(sbk), hd(sbv), hd(fk), hd(fv), logf, conv_n[:, 8 - (CONV_W - 1):], h_n[:, 0], rw3[:, t - 1], wkv_n)
    return x2.reshape(b, t, D_MODEL), new_state


def kernel(x_prompt, x_sample, cache_sb_k, cache_sb_v, cache_fox_k, cache_fox_v, cache_fox_logf, state_lru_conv, state_lru_h, state_rwkv_shift, state_rwkv_wkv, page_table, norm1_g, w_in, lru_conv_w, lru_conv_b, lru_wa, lru_ba, lru_wi, lru_bi, lru_lambda, fox_qnorm_g, fox_knorm_g, fox_fbias, rw_mu, rw_w0, rw_w2, rw_a0, rw_a2, rw_g2, rw_kk, rw_ka, rw_rk, rw_ln_g, rw_ln_b, w_branch, w_out, norm2_g, moe_wg, moe_bg, moe_we, moe_be, moe_w1, moe_w3, moe_w2):
    params = dict(norm1_g=norm1_g, w_in=w_in, lru_conv_w=lru_conv_w, lru_conv_b=lru_conv_b, lru_wa=lru_wa,
                  lru_ba=lru_ba, lru_wi=lru_wi, lru_bi=lru_bi, lru_lambda=lru_lambda, fox_qnorm_g=fox_qnorm_g,
                  fox_knorm_g=fox_knorm_g, fox_fbias=fox_fbias, rw_mu=rw_mu, rw_w0=rw_w0, rw_w2=rw_w2, rw_a0=rw_a0,
                  rw_a2=rw_a2, rw_g2=rw_g2, rw_kk=rw_kk, rw_ka=rw_ka, rw_rk=rw_rk, rw_ln_g=rw_ln_g, rw_ln_b=rw_ln_b,
                  w_branch=w_branch, w_out=w_out, norm2_g=norm2_g, moe_wg=moe_wg, moe_bg=moe_bg, moe_we=moe_we,
                  moe_be=moe_be, moe_w1=moe_w1, moe_w3=moe_w3, moe_w2=moe_w2)
    depth = w_in.shape[0]
    bp = x_prompt.shape[0]
    sb_kt, sb_vt = _pages_t(cache_sb_k), _pages_t(cache_sb_v)
    fox_kt, fox_vt = _pages_t(cache_fox_k), _pages_t(cache_fox_v)
    logf_pairs = _logf_pairs(cache_fox_logf)
    prompt_init = (jnp.zeros((bp, CONV_W - 1, BRANCH_W), F32), jnp.zeros((bp, BRANCH_W), F32),
                   jnp.zeros((bp, D_RW_IN), F32), jnp.zeros((bp, N_HEADS, HEAD_DIM, HEAD_DIM), F32))
    w_proj, w_gate = _regroup_w_in(w_in)
    y_p, y_s = x_prompt, x_sample
    st_p, st_s = [], []
    for l in range(depth):
        lw = _prep_layer(params, l, w_proj, w_gate)
        y_p, sp = _layer(y_p, lw, prompt_init, None)
        y_s, ss = _layer(y_s, lw, (state_lru_conv[l], state_lru_h[l], state_rwkv_shift[l], state_rwkv_wkv[l]),
                         (sb_kt, sb_vt, fox_kt, fox_vt, logf_pairs, page_table, l))
        st_p.append(sp)
        st_s.append(ss)
    stack = lambda sts: [jnp.stack([s[i] for s in sts], axis=0) for i in range(9)]
    return (y_p, y_s, *stack(st_p), *stack(st_s))
```

```python
import functools
import math

import jax
import jax.numpy as jnp
from jax import lax
from jax.experimental import pallas as pl
from jax.experimental.pallas import tpu as pltpu

F32 = jnp.float32
BF16 = jnp.bfloat16

D_MODEL = 1024
HEAD_DIM = 64
N_HEADS = 4
BRANCH_W = N_HEADS * HEAD_DIM
CONV_W = 4
LRU_C = 8.0
PAGE = 128
RW_TAIL = 128
D_RW_IN = 3 * BRANCH_W + RW_TAIL
N_MAIN = 8 * BRANCH_W
N_GROUPS = 4
N_EXPERTS = 16
D_EXP = 256
RMS_EPS = 1e-6
RW_GN_EPS = 64e-5
ATT_SCALE = HEAD_DIM ** -0.5
LANE = 128
RW_CHUNK = 64
VMEM_LIMIT = 56 * 1024 * 1024


def _cparams(sem):
    return pltpu.CompilerParams(dimension_semantics=sem, vmem_limit_bytes=VMEM_LIMIT)


def _dot(a, b):
    return jnp.dot(a.astype(BF16), b.astype(BF16), preferred_element_type=F32)


def _dot_nt(a, b):
    return lax.dot_general(a.astype(BF16), b.astype(BF16), (((1,), (1,)), ((), ())), preferred_element_type=F32)


def _dot_tn(a, b):
    return lax.dot_general(a.astype(BF16), b.astype(BF16), (((0,), (0,)), ((), ())), preferred_element_type=F32)


def _dot2(a, b_bf16):
    hi = a.astype(BF16)
    lo = (a - hi.astype(F32)).astype(BF16)
    return (jnp.dot(hi, b_bf16, preferred_element_type=F32) + jnp.dot(lo, b_bf16, preferred_element_type=F32))


def _softplus(x):
    return jnp.maximum(x, 0.0) + jnp.log1p(jnp.exp(-jnp.abs(x)))


def _sigmoid(x):
    return 1.0 / (1.0 + jnp.exp(-x))


def _gelu_tanh(x):
    return 0.5 * x * (1.0 + jnp.tanh(math.sqrt(2.0 / math.pi) * (x + 0.044715 * (x * x * x))))


def _rmsnorm(x, g):
    return x * lax.rsqrt(jnp.mean(x * x, axis=-1, keepdims=True) + RMS_EPS) * g


def _row_iota(shape):
    return lax.broadcasted_iota(jnp.int32, shape, 0)


def _col_iota(shape):
    return lax.broadcasted_iota(jnp.int32, shape, 1)


def _scan_rows(a, u):
    n = a.shape[0]
    row = _row_iota(a.shape)
    s = 1
    while s < n:
        valid = row >= s
        a_sh = pltpu.roll(a, s, 0)
        u_sh = pltpu.roll(u, s, 0)
        u = jnp.where(valid, a * u_sh + u, u)
        a = jnp.where(valid, a * a_sh, a)
        s *= 2
    return a, u


def _cumsum_rows(x):
    n = x.shape[0]
    row = _row_iota(x.shape)
    s = 1
    while s < n:
        x = x + jnp.where(row >= s, pltpu.roll(x, s, 0), 0.0)
        s *= 2
    return x


def _cumsum_lanes(x):
    ax = x.ndim - 1
    n = x.shape[ax]
    col = lax.broadcasted_iota(jnp.int32, x.shape, ax)
    s = 1
    while s < n:
        x = x + jnp.where(col >= s, pltpu.roll(x, s, ax), 0.0)
        s *= 2
    return x


def _head_ones():
    i = jnp.arange(BRANCH_W) // HEAD_DIM
    return (i[:, None] == i[None, :]).astype(BF16)


def _proj_kernel(x_ref, g_ref, w_ref, qg_ref, kg_ref, fb_ref, hs_ref,
                 lru_ref, rw_ref, sbq_ref, sbk_ref, sbv_ref, sbkb_ref, sbvb_ref,
                 fq_ref, fk_ref, fv_ref, fkb_ref, fvb_ref, logf_ref, *, state_t):
    xn = _rmsnorm(x_ref[...], g_ref[...]).astype(BF16)

    def seg(a, b):
        return jnp.dot(xn, w_ref[:, a:b], preferred_element_type=F32)

    def put_state(ref, val):
        if state_t:
            ref[0] = val.T
        else:
            ref[...] = val

    W = BRANCH_W
    lru_ref[...] = seg(0, 2 * W)
    sbq_ref[...] = (seg(2 * W, 3 * W) * ATT_SCALE).astype(BF16)
    k = seg(3 * W, 4 * W)
    put_state(sbk_ref, k)
    sbkb_ref[...] = k.astype(BF16)
    v = seg(4 * W, 5 * W)
    put_state(sbv_ref, v)
    sbvb_ref[...] = v.astype(BF16)
    hs = hs_ref[...]
    q = seg(5 * W, 6 * W)
    q = q * lax.rsqrt(_dot2(q * q, hs) * (1.0 / HEAD_DIM) + RMS_EPS) * qg_ref[...]
    fq_ref[...] = (q * ATT_SCALE).astype(BF16)
    k = seg(6 * W, 7 * W)
    k = k * lax.rsqrt(_dot2(k * k, hs) * (1.0 / HEAD_DIM) + RMS_EPS) * kg_ref[...]
    put_state(fk_ref, k)
    fkb_ref[...] = k.astype(BF16)
    v = seg(7 * W, 8 * W)
    put_state(fv_ref, v)
    ones = jnp.ones((v.shape[0], HEAD_DIM), F32)
    fvb_ref[...] = jnp.concatenate(
        [p for h in range(N_HEADS) for p in (v[:, h * HEAD_DIM:(h + 1) * HEAD_DIM], ones)], axis=1).astype(BF16)
    rw_ref[...] = seg(N_MAIN, N_MAIN + D_RW_IN)
    f = seg(N_MAIN + D_RW_IN, N_MAIN + D_RW_IN + LANE) + fb_ref[...]
    logf_ref[...] = -_softplus(-f)


def _proj(x2d, lw, tm, seq_len=None):
    n = x2d.shape[0]
    W = BRANCH_W
    row = lambda w: pl.BlockSpec((tm, w), lambda i: (i, 0))
    full = lambda a: pl.BlockSpec(a.shape, lambda i: (0,) * a.ndim)
    ins = (x2d, lw['norm1_g'], lw['w_proj'], lw['fox_qg'], lw['fox_kg'], lw['fox_fb'], lw['head_ones'])
    out_w = (2 * W, D_RW_IN, W, W, W, W, W, W, W, W, W, N_HEADS * LANE, LANE)
    out_dt = (F32, F32, BF16, F32, F32, BF16, BF16, BF16, F32, F32, BF16, BF16, F32)
    out_specs = [row(w) for w in out_w]
    out_shape = [jax.ShapeDtypeStruct((n, w), dt) for w, dt in zip(out_w, out_dt)]
    if seq_len is not None:
        per_seq = seq_len // tm
        for idx in (3, 4, 8, 9):
            out_specs[idx] = pl.BlockSpec((1, W, tm), lambda i: (i // per_seq, 0, i % per_seq))
            out_shape[idx] = jax.ShapeDtypeStruct((n // seq_len, W, seq_len), F32)
    layer = lw['layer']
    w_spec = pl.BlockSpec((None, D_MODEL, N_PROJ), lambda i: (layer, 0, 0))
    return pl.pallas_call(
        functools.partial(_proj_kernel, state_t=seq_len is not None),
        grid=(n // tm,),
        in_specs=[row(D_MODEL), full(ins[1]), w_spec] + [full(a) for a in ins[3:]],
        out_specs=out_specs,
        out_shape=out_shape,
        compiler_params=_cparams(("parallel",)),
        name="proj",
    )(*ins)


def _lru_kernel(xy_ref, conv0_ref, h0_ref, cw_ref, cb_ref, wa_ref, ba_ref, wi_ref, bi_ref, lam_ref,
                y_ref, convn_ref, hn_ref, tail_sc, h_sc, *, t_last):
    j = pl.program_id(1)
    W = BRANCH_W

    @pl.when(j == 0)
    def _():
        tail_sc[...] = conv0_ref[0]
        h_sc[...] = h0_ref[0]

    x = xy_ref[0, :, 0:W]
    y = xy_ref[0, :, W:2 * W]
    tt = x.shape[0]
    tail = tail_sc[...]
    row8 = _row_iota((8, W))
    cw = cw_ref[...]
    xc = cb_ref[...] + cw[CONV_W - 1:CONV_W, :] * x
    for kback in range(1, CONV_W):
        xs = pltpu.roll(x, kback, 0)
        head = jnp.where(row8 < kback, pltpu.roll(tail, kback, 0), xs[0:8])
        xs = head if tt == 8 else jnp.concatenate([head, xs[8:]], axis=0)
        xc = xc + cw[CONV_W - 1 - kback:CONV_W - kback, :] * xs
    r = _sigmoid(_dot(xc, wa_ref[...]) + ba_ref[...])
    ig = _sigmoid(_dot(xc, wi_ref[...]) + bi_ref[...])
    log_a = (-LRU_C) * r * _softplus(-lam_ref[...])
    a = jnp.exp(log_a)
    u = jnp.sqrt(-jnp.tanh(log_a) * (a * a + 1.0)) * (ig * xc)
    ap, hloc = _scan_rows(a, u)
    h = ap * h_sc[...] + hloc
    y_ref[0] = h * _gelu_tanh(y)
    h_sc[...] = h[tt - 1:tt, :]
    tail_sc[...] = x[tt - 8:tt, :]

    @pl.when(j == pl.num_programs(1) - 1)
    def _():
        hn_ref[0] = h[t_last - 1:t_last, :]
        convn_ref[0] = x[t_last - 8:t_last, :] if t_last >= 8 else jnp.where(
            row8 < 8 - t_last, pltpu.roll(tail, 8 - t_last, 0), pltpu.roll(x[0:8], 8 - t_last, 0))


def _lru(xy, conv0, h0, lw, tt, t_valid):
    b, tp, _ = xy.shape
    W = BRANCH_W
    nt = tp // tt
    t_last = t_valid - (nt - 1) * tt
    full = lambda a: pl.BlockSpec(a.shape, lambda i, j: (0,) * a.ndim)
    ws = (lw['lru_conv_w'], lw['lru_conv_b'], lw['lru_wa_bd'], lw['lru_ba'], lw['lru_wi_bd'], lw['lru_bi'], lw['lru_lambda'])
    return pl.pallas_call(
        functools.partial(_lru_kernel, t_last=t_last),
        grid=(b, nt),
        in_specs=[pl.BlockSpec((1, tt, 2 * W), lambda i, j: (i, j, 0)),
                  pl.BlockSpec((1, 8, W), lambda i, j: (i, 0, 0)),
                  pl.BlockSpec((1, 1, W), lambda i, j: (i, 0, 0))] + [full(a) for a in ws],
        out_specs=[pl.BlockSpec((1, tt, W), lambda i, j: (i, j, 0)),
                   pl.BlockSpec((1, 8, W), lambda i, j: (i, 0, 0)),
                   pl.BlockSpec((1, 1, W), lambda i, j: (i, 0, 0))],
        out_shape=[jax.ShapeDtypeStruct((b, tp, W), F32),
                   jax.ShapeDtypeStruct((b, 8, W), F32),
                   jax.ShapeDtypeStruct((b, 1, W), F32)],
        scratch_shapes=[pltpu.VMEM((8, W), F32), pltpu.VMEM((1, W), F32)],
        compiler_params=_cparams(("parallel", "arbitrary")),
        name="rglru",
    )(xy, conv0, h0, *ws)


def _block_diag(w):
    n, c, d = w.shape
    eye = jnp.eye(n, dtype=w.dtype)
    return (eye[:, None, :, None] * w[:, :, None, :]).reshape(n * c, n * d)


N_PROJ = N_MAIN + D_RW_IN + LANE
N_GATE = N_HEADS * D_MODEL


def _regroup_kernel(w_ref, wp_ref, wg_ref):
    o_f = N_MAIN
    o_rw = N_MAIN + N_HEADS
    o_g = o_rw + D_RW_IN
    rows = w_ref.shape[1]
    wp_ref[0, :, 0:o_f] = w_ref[0, :, 0:o_f].astype(BF16)
    wp_ref[0, :, o_f:o_f + D_RW_IN] = w_ref[0, :, o_rw:o_g].astype(BF16)
    f_cols = jnp.concatenate([w_ref[0, :, o_f:o_rw], jnp.zeros((rows, LANE - N_HEADS), F32)], axis=1)
    wp_ref[0, :, o_f + D_RW_IN:N_PROJ] = f_cols.astype(BF16)
    wg_ref[0] = w_ref[0, :, o_g:o_g + N_GATE].astype(BF16)


def _regroup_w_in(w_in):
    depth, d, n = w_in.shape
    rows = 256
    return pl.pallas_call(
        _regroup_kernel,
        grid=(depth, d // rows),
        in_specs=[pl.BlockSpec((1, rows, n), lambda l, i: (l, i, 0))],
        out_specs=[pl.BlockSpec((1, rows, N_PROJ), lambda l, i: (l, i, 0)),
                   pl.BlockSpec((1, rows, N_GATE), lambda l, i: (l, i, 0))],
        out_shape=[jax.ShapeDtypeStruct((depth, d, N_PROJ), BF16), jax.ShapeDtypeStruct((depth, d, N_GATE), BF16)],
        compiler_params=_cparams(("parallel", "parallel")),
        name="regroup_w_in",
    )(w_in)


def _prep_layer(p, l, w_proj, w_gate):
    row = lambda a: a[l].reshape(1, -1).astype(F32)
    lw = {
        'layer': l,
        'norm1_g': row(p['norm1_g']),
        'w_proj': w_proj,
        'w_gate': w_gate,
        'fox_qg': jnp.tile(p['fox_qnorm_g'][l], N_HEADS).reshape(1, -1),
        'fox_kg': jnp.tile(p['fox_knorm_g'][l], N_HEADS).reshape(1, -1),
        'fox_fb': jnp.pad(p['fox_fbias'][l], (0, LANE - N_HEADS)).reshape(1, -1),
        'head_ones': _head_ones(),
        'lru_conv_w': p['lru_conv_w'][l],
        'lru_conv_b': row(p['lru_conv_b']),
        'lru_wa_bd': _block_diag(p['lru_wa'][l]).astype(BF16),
        'lru_ba': row(p['lru_ba']),
        'lru_wi_bd': _block_diag(p['lru_wi'][l]).astype(BF16),
        'lru_bi': row(p['lru_bi']),
        'lru_lambda': row(p['lru_lambda']),
        'rw_mu': row(p['rw_mu']),
        'rw_w0': row(p['rw_w0']),
        'rw_a0': row(p['rw_a0']),
        'rw_w2p': jnp.pad(p['rw_w2'][l], ((0, 96), (0, 0))).astype(BF16),
        'rw_a2p': jnp.pad(p['rw_a2'][l], ((32, 64), (0, 0))).astype(BF16),
        'rw_g2p': jnp.pad(p['rw_g2'][l], ((64, 0), (0, 0))).astype(BF16),
        'rw_kk': row(p['rw_kk']),
        'rw_ka': row(p['rw_ka']),
        'rw_rk': row(p['rw_rk']),
        'rw_ln_g': row(p['rw_ln_g']),
        'rw_ln_b': row(p['rw_ln_b']),
        'w_branch': p['w_branch'][l].astype(BF16),
        'w_out': p['w_out'][l].astype(BF16),
        'norm2_g': row(p['norm2_g']),
        'moe_w1': p['moe_w1'][l].astype(BF16),
        'moe_w3': p['moe_w3'][l].astype(BF16),
        'moe_w2': p['moe_w2'][l].astype(BF16),
    }
    w_r = jnp.concatenate([p['moe_we'][l], p['moe_wg'][l]], axis=1)
    lw['moe_wr'] = jnp.pad(w_r, ((0, 0), (0, LANE - w_r.shape[1])))
    b_r = jnp.concatenate([p['moe_be'][l], p['moe_bg'][l]])
    lw['moe_br'] = jnp.pad(b_r, (0, LANE - b_r.shape[0])).reshape(1, -1)
    return lw


def _rwkv_kernel(pd_ref, shift0_ref, st0_ref, mu_ref, w0_ref, w2_ref, a0_ref, a2_ref, g2_ref,
                 kkw_ref, ka_ref, rk_ref, lng_ref, lnb_ref, hs_ref,
                 y_ref, stn_ref, prev_sc, st_sc, *, t_valid):
    j = pl.program_id(1)
    W = BRANCH_W
    nb, C = pd_ref.shape[0], pd_ref.shape[1]
    R4 = N_HEADS * C

    @pl.when(j == 0)
    def _():
        prev_sc[...] = shift0_ref[...]
        st_sc[...] = st0_ref[...]

    hs = hs_ref[...]
    row = _row_iota((C, 1))
    live = (row + j * C) < t_valid
    rr = _row_iota((R4, R4))
    cc = _col_iota((R4, R4))
    same_head = (rr // C) == (cc // C)
    strict = same_head & ((cc % C) < (rr % C))
    incl = same_head & ((cc % C) <= (rr % C))
    eye = rr == cc
    tile4 = lambda x: jnp.concatenate([x] * N_HEADS, axis=0)
    stack = lambda x: jnp.concatenate([x[:, h * HEAD_DIM:(h + 1) * HEAD_DIM] for h in range(N_HEADS)], axis=0)

    def chain(bi):
        pd = pd_ref[bi]
        prv = jnp.where(row == 0, prev_sc[bi], pltpu.roll(pd, 1, 0))
        ps = pd + (prv - pd) * mu_ref[...]
        r = ps[:, 0:W]
        k = ps[:, W:2 * W]
        v = ps[:, 2 * W:3 * W]
        tail = ps[:, 3 * W:3 * W + RW_TAIL]
        w_log = -_softplus(-(w0_ref[...] + _dot(jnp.tanh(tail), w2_ref[...]))) - 0.5
        logw = -jnp.exp(w_log)
        a = _sigmoid(a0_ref[...] + _dot(tail, a2_ref[...]))
        g = _dot(_sigmoid(tail), g2_ref[...])
        kk = k * kkw_ref[...]
        kk_ss = _dot2(kk * kk, hs)
        yield
        kk = kk / jnp.maximum(jnp.sqrt(kk_ss), 1e-12)
        k = k * (1.0 + (a - 1.0) * ka_ref[...])
        logw = jnp.where(live, logw, 0.0)
        kk = jnp.where(live, kk, 0.0)
        kl = jnp.where(live, k, 0.0)

        c = _cumsum_rows(logw)
        e_pos = jnp.exp(c)
        e_neg = jnp.exp(-c)
        e_end = e_pos[C - 1:C, :]
        a_bd = jnp.where(same_head, tile4(-kk * jnp.exp(c - logw)), 0.0)
        r_bd = jnp.where(same_head, tile4(r * e_pos), 0.0)
        b_t = kk * a * e_neg
        k_t = kl * e_neg
        ar = jnp.concatenate([a_bd, r_bd], axis=0).astype(BF16)
        bk = jnp.concatenate([tile4(b_t), tile4(k_t)], axis=0).astype(BF16)
        m = _dot_nt(ar, bk)
        st = st_sc[bi]
        ars = _dot(ar, st)
        v_st = stack(v)
        yield
        L = jnp.where(strict, m[0:R4, 0:R4], 0.0)
        X = ars[0:R4] + _dot(jnp.where(strict, m[0:R4, R4:2 * R4], 0.0), v_st)
        yield
        s = 1
        while s < C:
            s *= 2
            if s < C:
                lx = _dot(L, jnp.concatenate([L, X], axis=1))
                L, X = lx[:, 0:R4], X + lx[:, R4:R4 + HEAD_DIM]
            else:
                X = X + _dot(L, X)
            yield
        y_st = (ars[R4:2 * R4] + _dot(jnp.where(incl, m[R4:2 * R4, 0:R4], 0.0), X)
                + _dot(jnp.where(incl, m[R4:2 * R4, R4:2 * R4], 0.0), v_st))
        bg_bd = jnp.where(same_head, tile4(b_t * e_end), 0.0)
        kg_bd = jnp.where(same_head, tile4(k_t * e_end), 0.0)
        e_col = jnp.sum(jnp.where(eye, jnp.broadcast_to(e_end, (R4, R4)), 0.0), axis=1, keepdims=True)
        st_sc[bi] = st * e_col + _dot_tn(bg_bd, X) + _dot_tn(kg_bd, v_st)
        yield

        y = jnp.concatenate([y_st[h * C:(h + 1) * C] for h in range(N_HEADS)], axis=1)
        inv_d = 1.0 / HEAD_DIM
        mean = _dot2(y, hs) * inv_d
        dv = y - mean
        var = _dot2(dv * dv, hs) * inv_d
        yn = dv * lax.rsqrt(var + RW_GN_EPS) * lng_ref[...] + lnb_ref[...]
        bonus = _dot2(r * k * rk_ref[...], hs) * v
        y_ref[bi] = (yn + bonus) * g
        prev_sc[bi] = pd[C - 1:C, :]

    chains = [chain(bi) for bi in range(nb)]
    while chains:
        chains = [ch for ch in chains if next(ch, chains) is not chains]

    @pl.when(j == pl.num_programs(1) - 1)
    def _():
        stn_ref[...] = st_sc[...]


def _rwkv(pd, shift0, s0, lw, t_valid, nb):
    b, tp, _ = pd.shape
    W = BRANCH_W
    C = RW_CHUNK
    st0 = jnp.swapaxes(s0, 2, 3).reshape(b, W, HEAD_DIM)
    full = lambda a: pl.BlockSpec(a.shape, lambda i, j: (0,) * a.ndim)
    ws = (lw['rw_mu'], lw['rw_w0'], lw['rw_w2p'], lw['rw_a0'], lw['rw_a2p'], lw['rw_g2p'],
          lw['rw_kk'], lw['rw_ka'], lw['rw_rk'], lw['rw_ln_g'], lw['rw_ln_b'], lw['head_ones'])
    y, stn = pl.pallas_call(
        functools.partial(_rwkv_kernel, t_valid=t_valid),
        grid=(b // nb, tp // C),
        in_specs=[pl.BlockSpec((nb, C, D_RW_IN), lambda i, j: (i, j, 0)),
                  pl.BlockSpec((nb, 1, D_RW_IN), lambda i, j: (i, 0, 0)),
                  pl.BlockSpec((nb, W, HEAD_DIM), lambda i, j: (i, 0, 0))] + [full(a) for a in ws],
        out_specs=[pl.BlockSpec((nb, C, W), lambda i, j: (i, j, 0)),
                   pl.BlockSpec((nb, W, HEAD_DIM), lambda i, j: (i, 0, 0))],
        out_shape=[jax.ShapeDtypeStruct((b, tp, W), F32),
                   jax.ShapeDtypeStruct((b, W, HEAD_DIM), F32)],
        scratch_shapes=[pltpu.VMEM((nb, 1, D_RW_IN), F32), pltpu.VMEM((nb, W, HEAD_DIM), F32)],
        compiler_params=_cparams(("parallel", "arbitrary")),
        name="rwkv7",
    )(pd, shift0, st0, *ws)
    return y, jnp.swapaxes(stn.reshape(b, N_HEADS, HEAD_DIM, HEAD_DIM), 2, 3)


ATT_BLK = 256
FOX_KEY_BLK = 1024


def _suffix_ones(n):
    i = jnp.arange(n)
    return jnp.concatenate([(i[:, None] > i[None, :]), jnp.ones((n, n), bool)], axis=1).astype(BF16)


def _sb_prompt_kernel(q_ref, k_ref, v_ref, m_ref, o_ref, run_sc, acc_sc):
    i = pl.program_id(1)
    T = ATT_BLK
    msuf = m_ref[...]
    lower = _col_iota((T, T)) < _row_iota((T, T))
    heads = [slice(h * HEAD_DIM, (h + 1) * HEAD_DIM) for h in range(N_HEADS)]
    run_sc[...] = jnp.zeros_like(run_sc)
    acc_sc[...] = jnp.zeros_like(acc_sc)

    def block(kj, diag):
        start = pl.multiple_of(kj * T, T)
        zs = [_dot_nt(q_ref[0, :, sl], k_ref[0, pl.ds(start, T), sl]) for sl in heads]
        log_betas = [jnp.minimum(z, 0.0) - jnp.log(1.0 + jnp.exp(-jnp.abs(z))) for z in zs]
        lss = [lb - z for lb, z in zip(log_betas, zs)]
        if diag:
            lss = [jnp.where(lower, ls, 0.0) for ls in lss]
        sts = [_dot(ls, msuf) for ls in lss]
        ws = [jnp.exp(lb + st + run_sc[h]) for h, (lb, st) in enumerate(zip(log_betas, sts))]
        if diag:
            ws = [jnp.where(lower, w, 0.0) for w in ws]
        for h, sl in enumerate(heads):
            acc_sc[h] += _dot(ws[h], v_ref[0, pl.ds(start, T), sl])
            run_sc[h] += sts[h][:, 0:1] + lss[h][:, 0:1]

    block(i, True)

    def body(jj, carry):
        block(i - 1 - jj, False)
        return carry

    lax.fori_loop(0, i, body, 0)
    o_ref[0] = jnp.concatenate([acc_sc[h] for h in range(N_HEADS)], axis=1)


def _sb_prompt(q, k, v):
    b, t, W = q.shape
    msuf = _suffix_ones(ATT_BLK)[:, :ATT_BLK]
    return pl.pallas_call(
        _sb_prompt_kernel,
        grid=(b, t // ATT_BLK),
        in_specs=[pl.BlockSpec((1, ATT_BLK, W), lambda i, j: (i, j, 0)),
                  pl.BlockSpec((1, t, W), lambda i, j: (i, 0, 0)),
                  pl.BlockSpec((1, t, W), lambda i, j: (i, 0, 0)),
                  pl.BlockSpec(msuf.shape, lambda i, j: (0, 0))],
        out_specs=pl.BlockSpec((1, ATT_BLK, W), lambda i, j: (i, j, 0)),
        out_shape=jax.ShapeDtypeStruct((b, t, W), F32),
        scratch_shapes=[pltpu.VMEM((N_HEADS, ATT_BLK, 1), F32), pltpu.VMEM((N_HEADS, ATT_BLK, HEAD_DIM), F32)],
        compiler_params=_cparams(("parallel", "arbitrary")),
        name="sb_prompt",
    )(q, k, v, msuf)


def _fox_prompt_kernel(q_ref, k_ref, v_ref, fq_ref, fk_ref, o_ref, m_sc, acc_sc):
    i = pl.program_id(1)
    T = ATT_BLK
    TK = FOX_KEY_BLK
    last = (i * T) // TK
    heads = [slice(h * HEAD_DIM, (h + 1) * HEAD_DIM) for h in range(N_HEADS)]
    m_sc[...] = jnp.full_like(m_sc, -1e30)
    acc_sc[...] = jnp.zeros_like(acc_sc)

    def block(kj, diag):
        start = pl.multiple_of(kj * TK, TK)
        keys = pl.ds(start, TK)
        ss = [_dot_nt(q_ref[0, :, sl], k_ref[0, keys, sl]) + (fq_ref[0, :, h:h + 1] - fk_ref[0, h:h + 1, keys])
              for h, sl in enumerate(heads)]
        if diag:
            visible = (_col_iota((T, TK)) + start) <= (_row_iota((T, TK)) + i * T)
            ss = [jnp.where(visible, s, -jnp.inf) for s in ss]
        m_olds = [m_sc[h] for h in range(N_HEADS)]
        m_news = [jnp.maximum(m, jnp.max(s, axis=1, keepdims=True)) for m, s in zip(m_olds, ss)]
        ps = [jnp.exp(s - m) for s, m in zip(ss, m_news)]
        for h in range(N_HEADS):
            alpha = jnp.exp(m_olds[h] - m_news[h])
            acc_sc[h] = acc_sc[h] * alpha + _dot(ps[h], v_ref[0, keys, h * LANE:(h + 1) * LANE])
            m_sc[h] = m_news[h]

    block(last, True)

    def body(jj, carry):
        block(last - 1 - jj, False)
        return carry

    lax.fori_loop(0, last, body, 0)
    o_ref[0] = jnp.concatenate([acc_sc[h][:, 0:HEAD_DIM] / acc_sc[h][:, HEAD_DIM:LANE] for h in range(N_HEADS)],
                               axis=1)


def _fox_prompt(q, k, v_ones, f_col, f_row):
    b, t, W = q.shape
    return pl.pallas_call(
        _fox_prompt_kernel,
        grid=(b, t // ATT_BLK),
        in_specs=[pl.BlockSpec((1, ATT_BLK, W), lambda i, j: (i, j, 0)),
                  pl.BlockSpec((1, t, W), lambda i, j: (i, 0, 0)),
                  pl.BlockSpec((1, t, N_HEADS * LANE), lambda i, j: (i, 0, 0)),
                  pl.BlockSpec((1, ATT_BLK, 8), lambda i, j: (i, j, 0)),
                  pl.BlockSpec((1, 8, t), lambda i, j: (i, 0, 0))],
        out_specs=pl.BlockSpec((1, ATT_BLK, W), lambda i, j: (i, j, 0)),
        out_shape=jax.ShapeDtypeStruct((b, t, W), F32),
        scratch_shapes=[pltpu.VMEM((N_HEADS, ATT_BLK, 1), F32), pltpu.VMEM((N_HEADS, ATT_BLK, LANE), F32)],
        compiler_params=_cparams(("parallel", "arbitrary")),
        name="fox_prompt",
    )(q, k, v_ones, f_col, f_row)


def _cumsum_kernel(x_ref, o_ref):
    n = x_ref.shape[2] // LANE
    parts = [_cumsum_lanes(x_ref[0, :, c * LANE:(c + 1) * LANE]) for c in range(n)]
    carry = jnp.zeros((8, LANE), F32)
    for c, s in enumerate(parts):
        o_ref[0, :, c * LANE:(c + 1) * LANE] = s + carry
        carry = carry + jnp.broadcast_to(s[:, LANE - 1:LANE], s.shape)


def _cumsum_prompt(logf_row):
    b, _, t = logf_row.shape
    return pl.pallas_call(
        _cumsum_kernel,
        grid=(b,),
        in_specs=[pl.BlockSpec((1, 8, t), lambda i: (i, 0, 0))],
        out_specs=pl.BlockSpec((1, 8, t), lambda i: (i, 0, 0)),
        out_shape=jax.ShapeDtypeStruct((b, 8, t), F32),
        compiler_params=_cparams(("parallel",)),
        name="logf_cumsum",
    )(logf_row)


PAGES_PER_STEP = 8
PAGE_SLOTS = 3
QROWS = N_HEADS * 8


def _f_sample_kernel(pt_ref, pool_ref, new_ref, o_ref, x_sc):
    i = pl.program_id(0)
    n_pages = pt_ref.shape[1]
    n = n_pages + 1

    def gather(j, carry):
        p = pt_ref[i, j]
        tile = pool_ref[0, p // 2]
        x_sc[j] = jnp.where(p % 2 == 1, pltpu.roll(tile, N_HEADS, 0), tile)
        return carry

    lax.fori_loop(0, n_pages, gather, 0, unroll=8)
    x_sc[n_pages] = new_ref[0]
    c = _cumsum_lanes(x_sc[...])
    tot = jnp.broadcast_to(c[:, :, LANE - 1:LANE], c.shape)
    inc = tot
    s = 1
    while s < n:
        inc = inc + jnp.concatenate([jnp.zeros((s, 8, LANE), F32), inc[0:n - s]], axis=0)
        s *= 2
    o_ref[0] = c + (inc - tot)


def _f_sample(pool_pairs, logf_new, page_table, layer):
    b, n_pages = page_table.shape
    return pl.pallas_call(
        _f_sample_kernel,
        grid_spec=pltpu.PrefetchScalarGridSpec(
            num_scalar_prefetch=1,
            grid=(b,),
            in_specs=[pl.BlockSpec((1,) + pool_pairs.shape[1:], lambda i, pt: (layer, 0, 0, 0)),
                      pl.BlockSpec((1, 8, LANE), lambda i, pt: (i, 0, 0))],
            out_specs=pl.BlockSpec((1, n_pages + 1, 8, LANE), lambda i, pt: (i, 0, 0, 0)),
            scratch_shapes=[pltpu.VMEM((n_pages + 1, 8, LANE), F32)]),
        out_shape=jax.ShapeDtypeStruct((b, n_pages + 1, 8, LANE), F32),
        compiler_params=_cparams(("arbitrary",)),
        name="logf_cumsum_paged",
    )(page_table, pool_pairs, logf_new)


def _pages_t(cache):
    l, n, p, h, d = cache.shape
    return jnp.transpose(cache, (0, 1, 3, 4, 2)).reshape(l, n, h * d, p)


def _logf_pairs(cache_logf):
    l, n, p, h = cache_logf.shape
    return jnp.swapaxes(cache_logf, 2, 3).reshape(l, n // 2, 2 * h, p)


def _rows_t(x):
    return jnp.pad(jnp.swapaxes(x, 1, 2), ((0, 0), (0, 0), (0, LANE - x.shape[1])))


def _head_mask():
    return (jnp.arange(QROWS)[:, None] // 8) == (jnp.arange(BRANCH_W)[None, :] // HEAD_DIM)


def _heads_q(q):
    b, t, _ = q.shape
    q8 = jnp.pad(q, ((0, 0), (0, 8 - t), (0, 0)))
    return jnp.where(_head_mask(), jnp.tile(q8, (1, N_HEADS, 1)), 0).astype(BF16)


def _heads_out(o, t):
    b = o.shape[0]
    o4 = jnp.where(_head_mask(), o, 0.0).reshape(b, N_HEADS, 8, BRANCH_W)
    return jnp.sum(o4, axis=1)[:, :t, :]


def _sample_attn_kernel(pt_ref, qs_ref, qf_ref, fq_ref, ksn_ref, vsn_ref, kfn_ref, vfn_ref, f_ref, m_ref,
                        ck_sb, cv_sb, ck_fx, cv_fx, os_ref, of_ref,
                        kb_sb, vb_sb, kb_fx, vb_fx, sems, run_sc, accs_sc, m_sc, l_sc, accf_sc, *, layer, n_new):
    G = PAGES_PER_STEP
    b = pl.program_id(0)
    n_pages = pt_ref.shape[1]
    n_groups = n_pages // G
    caches = (ck_sb, cv_sb, ck_fx, cv_fx)
    bufs = (kb_sb, vb_sb, kb_fx, vb_fx)
    qs = qs_ref[0]
    qf = qf_ref[0]
    fq = fq_ref[0]
    msuf = m_ref[...]

    def page_of(grp, i):
        return n_pages - 1 - (grp * G + i)

    def copies(grp, slot):
        return [pltpu.make_async_copy(caches[a].at[layer, pt_ref[b, page_of(grp, i)]], bufs[a].at[slot, i],
                                      sems.at[slot, a])
                for a in range(len(caches)) for i in range(G)]

    def sb_blocks(kts, vts, mask):
        zs = [jnp.dot(qs, kt.astype(BF16), preferred_element_type=F32) for kt in kts]
        ls = [-_softplus(z) for z in zs]
        if mask is not None:
            ls = [jnp.where(mask, l, 0.0) for l in ls]
        st = _dot2(jnp.concatenate(ls, axis=0) if len(ls) > 1 else ls[0], msuf)
        run = run_sc[...]
        acc = accs_sc[...]
        for g in range(len(kts)):
            sg = st[g * QROWS:(g + 1) * QROWS]
            w = jnp.exp(zs[g] + ls[g] + sg[:, 0:LANE] + run)
            if mask is not None:
                w = jnp.where(mask, w, 0.0)
            acc = acc + _dot_nt(w, vts[g])
            run = run + sg[:, LANE:2 * LANE]
        run_sc[...] = run
        accs_sc[...] = acc

    def head_rows(f):
        return jnp.concatenate([jnp.broadcast_to(f[h:h + 1, :], (8, LANE)) for h in range(N_HEADS)], axis=0)

    def fox_blocks(kts, vts, fks, mask):
        ss = [jnp.dot(qf, kt.astype(BF16), preferred_element_type=F32) + (fq - head_rows(fk))
              for kt, fk in zip(kts, fks)]
        if mask is not None:
            ss = [jnp.where(mask, s, -jnp.inf) for s in ss]
        s_max = ss[0]
        for s in ss[1:]:
            s_max = jnp.maximum(s_max, s)
        m_old = m_sc[...]
        m_new = jnp.maximum(m_old, jnp.max(s_max, axis=1, keepdims=True))
        alpha = jnp.exp(m_old - m_new)
        ps = [jnp.exp(s - m_new) for s in ss]
        p_sum = ps[0]
        pv = _dot_nt(ps[0], vts[0])
        for p, vt in zip(ps[1:], vts[1:]):
            p_sum = p_sum + p
            pv = pv + _dot_nt(p, vt)
        l_sc[...] = l_sc[...] * alpha + jnp.sum(p_sum, axis=1, keepdims=True)
        accf_sc[...] = accf_sc[...] * alpha + pv
        m_sc[...] = m_new

    for grp in range(PAGE_SLOTS - 1):
        for c in copies(grp, grp):
            c.start()

    run_sc[...] = jnp.zeros_like(run_sc)
    accs_sc[...] = jnp.zeros_like(accs_sc)
    m_sc[...] = jnp.full_like(m_sc, -1e30)
    l_sc[...] = jnp.zeros_like(l_sc)
    accf_sc[...] = jnp.zeros_like(accf_sc)
    t = _row_iota((QROWS, LANE)) % 8
    s = _col_iota((QROWS, LANE))
    sb_blocks([ksn_ref[0]], [vsn_ref[0]], (s < t) & (s < n_new))
    fox_blocks([kfn_ref[0]], [vfn_ref[0]], [f_ref[0, n_pages]], (s <= t) & (s < n_new))

    def group(grp, carry):
        slot = grp % PAGE_SLOTS
        ahead = grp + (PAGE_SLOTS - 1)

        @pl.when(ahead < n_groups)
        def _():
            for c in copies(ahead, ahead % PAGE_SLOTS):
                c.start()

        for c in copies(grp, slot):
            c.wait()
        sb_blocks([kb_sb[slot, i] for i in range(G)], [vb_sb[slot, i] for i in range(G)], None)
        fox_blocks([kb_fx[slot, i] for i in range(G)], [vb_fx[slot, i] for i in range(G)],
                   [f_ref[0, page_of(grp, i)] for i in range(G)], None)
        return carry

    lax.fori_loop(0, n_groups, group, 0)
    os_ref[0] = accs_sc[...]
    of_ref[0] = accf_sc[...] / l_sc[...]


def _sample_attn(q_sb, k_sb, v_sb, q_fx, k_fx, v_fx, logf_new, caches, pool_pairs, page_table, layer):
    b, t, _ = q_sb.shape
    n_pages = page_table.shape[1]
    G = PAGES_PER_STEP
    lf = jnp.pad(jnp.swapaxes(logf_new, 1, 2), ((0, 0), (0, 8 - N_HEADS), (0, LANE - t)))
    f_all = _f_sample(pool_pairs, lf, page_table, layer)
    f_q = f_all[:, n_pages, :N_HEADS, :8]
    f_q = jnp.broadcast_to(f_q.reshape(b, QROWS, 1), (b, QROWS, LANE))
    msuf = _suffix_ones(LANE)
    q_spec = pl.BlockSpec((1, QROWS, BRANCH_W), lambda i, pt: (i, 0, 0))
    new_spec = pl.BlockSpec((1, BRANCH_W, LANE), lambda i, pt: (i, 0, 0))
    any_spec = pl.BlockSpec(memory_space=pl.ANY)
    page_buf = pltpu.VMEM((PAGE_SLOTS, G, BRANCH_W, PAGE), F32)
    o_sb, o_fx = pl.pallas_call(
        functools.partial(_sample_attn_kernel, layer=layer, n_new=t),
        grid_spec=pltpu.PrefetchScalarGridSpec(
            num_scalar_prefetch=1,
            grid=(b,),
            in_specs=[q_spec, q_spec, pl.BlockSpec((1, QROWS, LANE), lambda i, pt: (i, 0, 0)),
                      new_spec, new_spec, new_spec, new_spec,
                      pl.BlockSpec((1, n_pages + 1, 8, LANE), lambda i, pt: (i, 0, 0, 0)),
                      pl.BlockSpec(msuf.shape, lambda i, pt: (0, 0)),
                      any_spec, any_spec, any_spec, any_spec],
            out_specs=[q_spec, q_spec],
            scratch_shapes=[page_buf, page_buf, page_buf, page_buf, pltpu.SemaphoreType.DMA((PAGE_SLOTS, 4)),
                            pltpu.VMEM((QROWS, LANE), F32), pltpu.VMEM((QROWS, BRANCH_W), F32),
                            pltpu.VMEM((QROWS, 1), F32), pltpu.VMEM((QROWS, 1), F32),
                            pltpu.VMEM((QROWS, BRANCH_W), F32)]),
        out_shape=[jax.ShapeDtypeStruct((b, QROWS, BRANCH_W), F32)] * 2,
        compiler_params=_cparams(("arbitrary",)),
        name="sample_attn",
    )(page_table, _heads_q(q_sb), _heads_q(q_fx), f_q, _rows_t(k_sb), _rows_t(v_sb), _rows_t(k_fx), _rows_t(v_fx),
      f_all, msuf, *caches)
    return _heads_out(o_sb, t), _heads_out(o_fx, t)


def _merge_kernel(x_ref, ya_ref, yb_ref, yc_ref, yd_ref, g_ref, wg_ref, wb_ref, wo_ref, o_ref):
    x = x_ref[...]
    xn = _rmsnorm(x, g_ref[...]).astype(BF16)
    m = None
    for n, y_ref in enumerate((ya_ref, yb_ref, yc_ref, yd_ref)):
        gate = _sigmoid(jnp.dot(xn, wg_ref[:, n * D_MODEL:(n + 1) * D_MODEL], preferred_element_type=F32))
        term = gate * _dot(y_ref[...], wb_ref[n])
        m = term if m is None else m + term
    o_ref[...] = x + _dot(m, wo_ref[...])


def _merge(x2d, ys, lw, tm):
    n = x2d.shape[0]
    row = lambda w: pl.BlockSpec((tm, w), lambda i: (i, 0))
    full = lambda a: pl.BlockSpec(a.shape, lambda i: (0,) * a.ndim)
    ws = (lw['norm1_g'], lw['w_gate'], lw['w_branch'], lw['w_out'])
    layer = lw['layer']
    wg_spec = pl.BlockSpec((None, D_MODEL, N_GATE), lambda i: (layer, 0, 0))
    return pl.pallas_call(
        _merge_kernel,
        grid=(n // tm,),
        in_specs=[row(D_MODEL)] + [row(BRANCH_W)] * 4 + [full(ws[0]), wg_spec, full(ws[2]), full(ws[3])],
        out_specs=row(D_MODEL),
        out_shape=jax.ShapeDtypeStruct((n, D_MODEL), F32),
        compiler_params=_cparams(("parallel",)),
        name="merge",
    )(x2d, *ys, *ws)


MOE_EXPERTS_PER_STEP = 2


def _router_gate(logits):
    lane = _col_iota(logits.shape)
    lane_f = lane.astype(F32)
    big = float(LANE)
    neg = -jnp.inf
    gl = jnp.where((lane >= N_EXPERTS) & (lane < N_EXPERTS + N_GROUPS), logits, neg)
    g_max = jnp.max(gl, axis=1, keepdims=True)
    g_idx = jnp.min(jnp.where(gl == g_max, lane_f, big), axis=1, keepdims=True) - float(N_EXPERTS)
    g_w = 1.0 / jnp.sum(jnp.exp(gl - g_max), axis=1, keepdims=True)
    per_group = N_EXPERTS // N_GROUPS
    lo = g_idx * float(per_group)
    el = jnp.where((lane_f >= lo) & (lane_f < lo + float(per_group)), logits, neg)
    m1 = jnp.max(el, axis=1, keepdims=True)
    i1 = jnp.min(jnp.where(el == m1, lane_f, big), axis=1, keepdims=True)
    el2 = jnp.where(lane_f == i1, neg, el)
    m2 = jnp.max(el2, axis=1, keepdims=True)
    i2 = jnp.min(jnp.where(el2 == m2, lane_f, big), axis=1, keepdims=True)
    d = jnp.exp(m2 - m1)
    w1 = g_w / (1.0 + d)
    return jnp.where(lane_f == i1, w1, 0.0) + jnp.where(lane_f == i2, w1 * d, 0.0)


def _moe_kernel(x_ref, g_ref, wr_ref, br_ref, w1_ref, w3_ref, w2_ref, o_ref, xn_sc, gate_sc, acc_sc):
    e = pl.program_id(1)

    @pl.when(e == 0)
    def _():
        x = x_ref[...]
        xn = _rmsnorm(x, g_ref[...])
        xn_sc[...] = xn.astype(BF16)
        logits = jnp.dot(xn, wr_ref[...], preferred_element_type=F32, precision=lax.Precision.HIGHEST) + br_ref[...]
        gate_sc[...] = _router_gate(logits)
        acc_sc[...] = x

    xn = xn_sc[...]
    n_e = w1_ref.shape[0]
    gate = gate_sc[...]
    lane = _col_iota(gate.shape)
    a_s = [jnp.dot(xn, w1_ref[k], preferred_element_type=F32) for k in range(n_e)]
    b_s = [jnp.dot(xn, w3_ref[k], preferred_element_type=F32) for k in range(n_e)]
    g_s = [jnp.sum(jnp.where(lane == e * n_e + k, gate, 0.0), axis=1, keepdims=True) for k in range(n_e)]
    h = jnp.concatenate([(a * _sigmoid(a) * b * g).astype(BF16) for a, b, g in zip(a_s, b_s, g_s)], axis=1)
    acc_sc[...] += jnp.dot(h, w2_ref[...].reshape(n_e * D_EXP, D_MODEL), preferred_element_type=F32)

    @pl.when(e == pl.num_programs(1) - 1)
    def _():
        o_ref[...] = acc_sc[...]


def _moe(x2d, lw, tm):
    n = x2d.shape[0]
    row = pl.BlockSpec((tm, D_MODEL), lambda i, e: (i, 0))
    full = lambda a: pl.BlockSpec(a.shape, lambda i, e: (0,) * a.ndim)
    return pl.pallas_call(
        _moe_kernel,
        grid=(n // tm, N_EXPERTS // MOE_EXPERTS_PER_STEP),
        in_specs=[row, full(lw['norm2_g']), full(lw['moe_wr']), full(lw['moe_br']),
                  pl.BlockSpec((MOE_EXPERTS_PER_STEP, D_MODEL, D_EXP), lambda i, e: (e, 0, 0)),
                  pl.BlockSpec((MOE_EXPERTS_PER_STEP, D_MODEL, D_EXP), lambda i, e: (e, 0, 0)),
                  pl.BlockSpec((MOE_EXPERTS_PER_STEP, D_EXP, D_MODEL), lambda i, e: (e, 0, 0))],
        out_specs=row,
        out_shape=jax.ShapeDtypeStruct((n, D_MODEL), F32),
        scratch_shapes=[pltpu.VMEM((tm, D_MODEL), BF16), pltpu.VMEM((tm, LANE), F32), pltpu.VMEM((tm, D_MODEL), F32)],
        compiler_params=_cparams(("parallel", "arbitrary")),
        name="moe",
    )(x2d, lw['norm2_g'], lw['moe_wr'], lw['moe_br'], lw['moe_w1'], lw['moe_w3'], lw['moe_w2'])


def _pad_rows(a, mult):
    t = a.shape[1]
    tp = -(-t // mult) * mult
    return a if tp == t else jnp.pad(a, ((0, 0), (0, tp - t), (0, 0)))


def _layer(x, lw, state, paged):
    b, t, _ = x.shape
    n = b * t
    W = BRANCH_W
    conv0, h0, shift0, wkv0 = state
    tm = min(n, 256)
    x2d = x.reshape(n, D_MODEL)
    state_t = paged is None and t % tm == 0
    (lru_xy, rw_in, sbq, sbk, sbv, sbkb, sbvb, fq, fk, fv, fkb, fvb, logf_pad) = _proj(
        x2d, lw, tm, t if state_t else None)
    r3 = lambda a: a.reshape(b, t, a.shape[-1])
    logf = r3(logf_pad[:, :N_HEADS])

    tt = 256 if t % 256 == 0 else 8
    y_a, conv_n, h_n = _lru(_pad_rows(r3(lru_xy), tt), jnp.pad(conv0, ((0, 0), (8 - (CONV_W - 1), 0), (0, 0))),
                            h0[:, None, :], lw, tt, t)
    rw3 = r3(rw_in)
    y_d, wkv_n = _rwkv(_pad_rows(rw3, RW_CHUNK), shift0[:, None, :], wkv0, lw, t, 4)

    if paged is None:
        y_b = _sb_prompt(r3(sbq), r3(sbkb), r3(sbvb))
        f_row = _cumsum_prompt(jnp.pad(jnp.swapaxes(logf, 1, 2), ((0, 0), (0, 8 - N_HEADS), (0, 0))))
        y_c = _fox_prompt(r3(fq), r3(fkb), r3(fvb), jnp.swapaxes(f_row, 1, 2), f_row)
    else:
        sb_kt, sb_vt, fox_kt, fox_vt, logf_pairs, page_table, layer = paged
        y_b, y_c = _sample_attn(r3(sbq), r3(sbk), r3(sbv), r3(fq), r3(fk), r3(fv), logf,
                                (sb_kt, sb_vt, fox_kt, fox_vt), logf_pairs, page_table, layer)

    ys = (y_a[:, :t].reshape(n, W), y_b.reshape(n, W), y_c.reshape(n, W), y_d[:, :t].reshape(n, W))
    x1 = _merge(x2d, ys, lw, tm)
    x2 = _moe(x1, lw, min(n, 1024))
    if state_t:
        hd = lambda a: jnp.transpose(a.reshape(b, N_HEADS, HEAD_DIM, t), (0, 3, 1, 2))
    else:
        hd = lambda a: a.reshape(b, t, N_HEADS, HEAD_DIM)
    new_state = (hd(sbk), hd(sbv), hd(fk), hd(fv), logf, conv_n[:, 8 - (CONV_W - 1):], h_n[:, 0], rw3[:, t - 1], wkv_n)
    return x2.reshape(b, t, D_MODEL), new_state


def kernel(x_prompt, x_sample, cache_sb_k, cache_sb_v, cache_fox_k, cache_fox_v, cache_fox_logf, state_lru_conv, state_lru_h, state_rwkv_shift, state_rwkv_wkv, page_table, norm1_g, w_in, lru_conv_w, lru_conv_b, lru_wa, lru_ba, lru_wi, lru_bi, lru_lambda, fox_qnorm_g, fox_knorm_g, fox_fbias, rw_mu, rw_w0, rw_w2, rw_a0, rw_a2, rw_g2, rw_kk, rw_ka, rw_rk, rw_ln_g, rw_ln_b, w_branch, w_out, norm2_g, moe_wg, moe_bg, moe_we, moe_be, moe_w1, moe_w3, moe_w2):
    params = dict(norm1_g=norm1_g, w_in=w_in, lru_conv_w=lru_conv_w, lru_conv_b=lru_conv_b, lru_wa=lru_wa,
                  lru_ba=lru_ba, lru_wi=lru_wi, lru_bi=lru_bi, lru_lambda=lru_lambda, fox_qnorm_g=fox_qnorm_g,
                  fox_knorm_g=fox_knorm_g, fox_fbias=fox_fbias, rw_mu=rw_mu, rw_w0=rw_w0, rw_w2=rw_w2, rw_a0=rw_a0,
                  rw_a2=rw_a2, rw_g2=rw_g2, rw_kk=rw_kk, rw_ka=rw_ka, rw_rk=rw_rk, rw_ln_g=rw_ln_g, rw_ln_b=rw_ln_b,
                  w_branch=w_branch, w_out=w_out, norm2_g=norm2_g, moe_wg=moe_wg, moe_bg=moe_bg, moe_we=moe_we,
                  moe_be=moe_be, moe_w1=moe_w1, moe_w3=moe_w3, moe_w2=moe_w2)
    depth = w_in.shape[0]
    bp = x_prompt.shape[0]
    sb_kt, sb_vt = _pages_t(cache_sb_k), _pages_t(cache_sb_v)
    fox_kt, fox_vt = _pages_t(cache_fox_k), _pages_t(cache_fox_v)
    logf_pairs = _logf_pairs(cache_fox_logf)
    prompt_init = (jnp.zeros((bp, CONV_W - 1, BRANCH_W), F32), jnp.zeros((bp, BRANCH_W), F32),
                   jnp.zeros((bp, D_RW_IN), F32), jnp.zeros((bp, N_HEADS, HEAD_DIM, HEAD_DIM), F32))
    w_proj, w_gate = _regroup_w_in(w_in)
    y_p, y_s = x_prompt, x_sample
    st_p, st_s = [], []
    for l in range(depth):
        lw = _prep_layer(params, l, w_proj, w_gate)
        y_p, sp = _layer(y_p, lw, prompt_init, None)
        y_s, ss = _layer(y_s, lw, (state_lru_conv[l], state_lru_h[l], state_rwkv_shift[l], state_rwkv_wkv[l]),
                         (sb_kt, sb_vt, fox_kt, fox_vt, logf_pairs, page_table, l))
        st_p.append(sp)
        st_s.append(ss)
    stack = lambda sts: [jnp.stack([s[i] for s in sts], axis=0) for i in range(9)]
    return (y_p, y_s, *stack(st_p), *stack(st_s))
```

```python
import functools
import math

import jax
import jax.numpy as jnp
from jax import lax
from jax.experimental import pallas as pl
from jax.experimental.pallas import tpu as pltpu

F32 = jnp.float32
BF16 = jnp.bfloat16

D_MODEL = 1024
HEAD_DIM = 64
N_HEADS = 4
BRANCH_W = N_HEADS * HEAD_DIM
CONV_W = 4
LRU_C = 8.0
PAGE = 128
RW_TAIL = 128
D_RW_IN = 3 * BRANCH_W + RW_TAIL
N_MAIN = 8 * BRANCH_W
N_GROUPS = 4
N_EXPERTS = 16
D_EXP = 256
RMS_EPS = 1e-6
RW_GN_EPS = 64e-5
ATT_SCALE = HEAD_DIM ** -0.5
LANE = 128
RW_CHUNK = 64
VMEM_LIMIT = 56 * 1024 * 1024


def _cparams(sem):
    return pltpu.CompilerParams(dimension_semantics=sem, vmem_limit_bytes=VMEM_LIMIT)


def _dot(a, b):
    return jnp.dot(a.astype(BF16), b.astype(BF16), preferred_element_type=F32)


def _dot_nt(a, b):
    return lax.dot_general(a.astype(BF16), b.astype(BF16), (((1,), (1,)), ((), ())), preferred_element_type=F32)


def _dot_tn(a, b):
    return lax.dot_general(a.astype(BF16), b.astype(BF16), (((0,), (0,)), ((), ())), preferred_element_type=F32)


def _dot2(a, b_bf16):
    hi = a.astype(BF16)
    lo = (a - hi.astype(F32)).astype(BF16)
    return (jnp.dot(hi, b_bf16, preferred_element_type=F32) + jnp.dot(lo, b_bf16, preferred_element_type=F32))


def _softplus(x):
    return jnp.maximum(x, 0.0) + jnp.log1p(jnp.exp(-jnp.abs(x)))


def _sigmoid(x):
    return 1.0 / (1.0 + jnp.exp(-x))


def _gelu_tanh(x):
    return 0.5 * x * (1.0 + jnp.tanh(math.sqrt(2.0 / math.pi) * (x + 0.044715 * (x * x * x))))


def _rmsnorm(x, g):
    return x * lax.rsqrt(jnp.mean(x * x, axis=-1, keepdims=True) + RMS_EPS) * g


def _row_iota(shape):
    return lax.broadcasted_iota(jnp.int32, shape, 0)


def _col_iota(shape):
    return lax.broadcasted_iota(jnp.int32, shape, 1)


def _scan_rows(a, u):
    n = a.shape[0]
    row = _row_iota(a.shape)
    s = 1
    while s < n:
        valid = row >= s
        a_sh = pltpu.roll(a, s, 0)
        u_sh = pltpu.roll(u, s, 0)
        u = jnp.where(valid, a * u_sh + u, u)
        a = jnp.where(valid, a * a_sh, a)
        s *= 2
    return a, u


def _cumsum_rows(x):
    n = x.shape[0]
    row = _row_iota(x.shape)
    s = 1
    while s < n:
        x = x + jnp.where(row >= s, pltpu.roll(x, s, 0), 0.0)
        s *= 2
    return x


def _cumsum_lanes(x):
    ax = x.ndim - 1
    n = x.shape[ax]
    col = lax.broadcasted_iota(jnp.int32, x.shape, ax)
    s = 1
    while s < n:
        x = x + jnp.where(col >= s, pltpu.roll(x, s, ax), 0.0)
        s *= 2
    return x


def _head_ones():
    i = jnp.arange(BRANCH_W) // HEAD_DIM
    return (i[:, None] == i[None, :]).astype(BF16)


def _proj_kernel(x_ref, g_ref, w_ref, qg_ref, kg_ref, fb_ref, hs_ref,
                 lru_ref, rw_ref, sbq_ref, sbk_ref, sbv_ref, sbkb_ref, sbvb_ref,
                 fq_ref, fk_ref, fv_ref, fkb_ref, fvb_ref, logf_ref, *, state_t):
    xn = _rmsnorm(x_ref[...], g_ref[...]).astype(BF16)

    def seg(a, b):
        return jnp.dot(xn, w_ref[:, a:b], preferred_element_type=F32)

    def put_state(ref, val):
        if state_t:
            ref[0] = val.T
        else:
            ref[...] = val

    W = BRANCH_W
    lru_ref[...] = seg(0, 2 * W)
    sbq_ref[...] = (seg(2 * W, 3 * W) * ATT_SCALE).astype(BF16)
    k = seg(3 * W, 4 * W)
    put_state(sbk_ref, k)
    sbkb_ref[...] = k.astype(BF16)
    v = seg(4 * W, 5 * W)
    put_state(sbv_ref, v)
    sbvb_ref[...] = v.astype(BF16)
    hs = hs_ref[...]
    q = seg(5 * W, 6 * W)
    q = q * lax.rsqrt(_dot2(q * q, hs) * (1.0 / HEAD_DIM) + RMS_EPS) * qg_ref[...]
    fq_ref[...] = (q * ATT_SCALE).astype(BF16)
    k = seg(6 * W, 7 * W)
    k = k * lax.rsqrt(_dot2(k * k, hs) * (1.0 / HEAD_DIM) + RMS_EPS) * kg_ref[...]
    put_state(fk_ref, k)
    fkb_ref[...] = k.astype(BF16)
    v = seg(7 * W, 8 * W)
    put_state(fv_ref, v)
    ones = jnp.ones((v.shape[0], HEAD_DIM), F32)
    fvb_ref[...] = jnp.concatenate(
        [p for h in range(N_HEADS) for p in (v[:, h * HEAD_DIM:(h + 1) * HEAD_DIM], ones)], axis=1).astype(BF16)
    rw_ref[...] = seg(N_MAIN, N_MAIN + D_RW_IN)
    f = seg(N_MAIN + D_RW_IN, N_MAIN + D_RW_IN + LANE) + fb_ref[...]
    logf_ref[...] = -_softplus(-f)


def _proj(x2d, lw, tm, seq_len=None):
    n = x2d.shape[0]
    W = BRANCH_W
    row = lambda w: pl.BlockSpec((tm, w), lambda i: (i, 0))
    full = lambda a: pl.BlockSpec(a.shape, lambda i: (0,) * a.ndim)
    ins = (x2d, lw['norm1_g'], lw['w_proj'], lw['fox_qg'], lw['fox_kg'], lw['fox_fb'], lw['head_ones'])
    out_w = (2 * W, D_RW_IN, W, W, W, W, W, W, W, W, W, N_HEADS * LANE, LANE)
    out_dt = (F32, F32, BF16, F32, F32, BF16, BF16, BF16, F32, F32, BF16, BF16, F32)
    out_specs = [row(w) for w in out_w]
    out_shape = [jax.ShapeDtypeStruct((n, w), dt) for w, dt in zip(out_w, out_dt)]
    if seq_len is not None:
        per_seq = seq_len // tm
        for idx in (3, 4, 8, 9):
            out_specs[idx] = pl.BlockSpec((1, W, tm), lambda i: (i // per_seq, 0, i % per_seq))
            out_shape[idx] = jax.ShapeDtypeStruct((n // seq_len, W, seq_len), F32)
    layer = lw['layer']
    w_spec = pl.BlockSpec((None, D_MODEL, N_PROJ), lambda i: (layer, 0, 0))
    return pl.pallas_call(
        functools.partial(_proj_kernel, state_t=seq_len is not None),
        grid=(n // tm,),
        in_specs=[row(D_MODEL), full(ins[1]), w_spec] + [full(a) for a in ins[3:]],
        out_specs=out_specs,
        out_shape=out_shape,
        compiler_params=_cparams(("parallel",)),
        name="proj",
    )(*ins)


def _lru_kernel(xy_ref, conv0_ref, h0_ref, cw_ref, cb_ref, wa_ref, ba_ref, wi_ref, bi_ref, lam_ref,
                y_ref, convn_ref, hn_ref, tail_sc, h_sc, *, t_last):
    j = pl.program_id(1)
    W = BRANCH_W

    @pl.when(j == 0)
    def _():
        tail_sc[...] = conv0_ref[0]
        h_sc[...] = h0_ref[0]

    x = xy_ref[0, :, 0:W]
    y = xy_ref[0, :, W:2 * W]
    tt = x.shape[0]
    tail = tail_sc[...]
    row8 = _row_iota((8, W))
    cw = cw_ref[...]
    xc = cb_ref[...] + cw[CONV_W - 1:CONV_W, :] * x
    for kback in range(1, CONV_W):
        xs = pltpu.roll(x, kback, 0)
        head = jnp.where(row8 < kback, pltpu.roll(tail, kback, 0), xs[0:8])
        xs = head if tt == 8 else jnp.concatenate([head, xs[8:]], axis=0)
        xc = xc + cw[CONV_W - 1 - kback:CONV_W - kback, :] * xs
    r = _sigmoid(_dot(xc, wa_ref[...]) + ba_ref[...])
    ig = _sigmoid(_dot(xc, wi_ref[...]) + bi_ref[...])
    log_a = (-LRU_C) * r * _softplus(-lam_ref[...])
    a = jnp.exp(log_a)
    u = jnp.sqrt(-jnp.tanh(log_a) * (a * a + 1.0)) * (ig * xc)
    ap, hloc = _scan_rows(a, u)
    h = ap * h_sc[...] + hloc
    y_ref[0] = h * _gelu_tanh(y)
    h_sc[...] = h[tt - 1:tt, :]
    tail_sc[...] = x[tt - 8:tt, :]

    @pl.when(j == pl.num_programs(1) - 1)
    def _():
        hn_ref[0] = h[t_last - 1:t_last, :]
        convn_ref[0] = x[t_last - 8:t_last, :] if t_last >= 8 else jnp.where(
            row8 < 8 - t_last, pltpu.roll(tail, 8 - t_last, 0), pltpu.roll(x[0:8], 8 - t_last, 0))


def _lru(xy, conv0, h0, lw, tt, t_valid):
    b, tp, _ = xy.shape
    W = BRANCH_W
    nt = tp // tt
    t_last = t_valid - (nt - 1) * tt
    full = lambda a: pl.BlockSpec(a.shape, lambda i, j: (0,) * a.ndim)
    ws = (lw['lru_conv_w'], lw['lru_conv_b'], lw['lru_wa_bd'], lw['lru_ba'], lw['lru_wi_bd'], lw['lru_bi'], lw['lru_lambda'])
    return pl.pallas_call(
        functools.partial(_lru_kernel, t_last=t_last),
        grid=(b, nt),
        in_specs=[pl.BlockSpec((1, tt, 2 * W), lambda i, j: (i, j, 0)),
                  pl.BlockSpec((1, 8, W), lambda i, j: (i, 0, 0)),
                  pl.BlockSpec((1, 1, W), lambda i, j: (i, 0, 0))] + [full(a) for a in ws],
        out_specs=[pl.BlockSpec((1, tt, W), lambda i, j: (i, j, 0)),
                   pl.BlockSpec((1, 8, W), lambda i, j: (i, 0, 0)),
                   pl.BlockSpec((1, 1, W), lambda i, j: (i, 0, 0))],
        out_shape=[jax.ShapeDtypeStruct((b, tp, W), F32),
                   jax.ShapeDtypeStruct((b, 8, W), F32),
                   jax.ShapeDtypeStruct((b, 1, W), F32)],
        scratch_shapes=[pltpu.VMEM((8, W), F32), pltpu.VMEM((1, W), F32)],
        compiler_params=_cparams(("parallel", "arbitrary")),
        name="rglru",
    )(xy, conv0, h0, *ws)


def _block_diag(w):
    n, c, d = w.shape
    eye = jnp.eye(n, dtype=w.dtype)
    return (eye[:, None, :, None] * w[:, :, None, :]).reshape(n * c, n * d)


N_PROJ = N_MAIN + D_RW_IN + LANE
N_GATE = N_HEADS * D_MODEL


def _regroup_kernel(w_ref, wp_ref, wg_ref):
    o_f = N_MAIN
    o_rw = N_MAIN + N_HEADS
    o_g = o_rw + D_RW_IN
    rows = w_ref.shape[1]
    wp_ref[0, :, 0:o_f] = w_ref[0, :, 0:o_f].astype(BF16)
    wp_ref[0, :, o_f:o_f + D_RW_IN] = w_ref[0, :, o_rw:o_g].astype(BF16)
    f_cols = jnp.concatenate([w_ref[0, :, o_f:o_rw], jnp.zeros((rows, LANE - N_HEADS), F32)], axis=1)
    wp_ref[0, :, o_f + D_RW_IN:N_PROJ] = f_cols.astype(BF16)
    wg_ref[0] = w_ref[0, :, o_g:o_g + N_GATE].astype(BF16)


def _regroup_w_in(w_in):
    depth, d, n = w_in.shape
    rows = 256
    return pl.pallas_call(
        _regroup_kernel,
        grid=(depth, d // rows),
        in_specs=[pl.BlockSpec((1, rows, n), lambda l, i: (l, i, 0))],
        out_specs=[pl.BlockSpec((1, rows, N_PROJ), lambda l, i: (l, i, 0)),
                   pl.BlockSpec((1, rows, N_GATE), lambda l, i: (l, i, 0))],
        out_shape=[jax.ShapeDtypeStruct((depth, d, N_PROJ), BF16), jax.ShapeDtypeStruct((depth, d, N_GATE), BF16)],
        compiler_params=_cparams(("parallel", "parallel")),
        name="regroup_w_in",
    )(w_in)


def _prep_layer(p, l, w_proj, w_gate):
    row = lambda a: a[l].reshape(1, -1).astype(F32)
    lw = {
        'layer': l,
        'norm1_g': row(p['norm1_g']),
        'w_proj': w_proj,
        'w_gate': w_gate,
        'fox_qg': jnp.tile(p['fox_qnorm_g'][l], N_HEADS).reshape(1, -1),
        'fox_kg': jnp.tile(p['fox_knorm_g'][l], N_HEADS).reshape(1, -1),
        'fox_fb': jnp.pad(p['fox_fbias'][l], (0, LANE - N_HEADS)).reshape(1, -1),
        'head_ones': _head_ones(),
        'lru_conv_w': p['lru_conv_w'][l],
        'lru_conv_b': row(p['lru_conv_b']),
        'lru_wa_bd': _block_diag(p['lru_wa'][l]).astype(BF16),
        'lru_ba': row(p['lru_ba']),
        'lru_wi_bd': _block_diag(p['lru_wi'][l]).astype(BF16),
        'lru_bi': row(p['lru_bi']),
        'lru_lambda': row(p['lru_lambda']),
        'rw_mu': row(p['rw_mu']),
        'rw_w0': row(p['rw_w0']),
        'rw_a0': row(p['rw_a0']),
        'rw_w2p': jnp.pad(p['rw_w2'][l], ((0, 96), (0, 0))).astype(BF16),
        'rw_a2p': jnp.pad(p['rw_a2'][l], ((32, 64), (0, 0))).astype(BF16),
        'rw_g2p': jnp.pad(p['rw_g2'][l], ((64, 0), (0, 0))).astype(BF16),
        'rw_kk': row(p['rw_kk']),
        'rw_ka': row(p['rw_ka']),
        'rw_rk': row(p['rw_rk']),
        'rw_ln_g': row(p['rw_ln_g']),
        'rw_ln_b': row(p['rw_ln_b']),
        'w_branch': p['w_branch'][l].astype(BF16),
        'w_out': p['w_out'][l].astype(BF16),
        'norm2_g': row(p['norm2_g']),
        'moe_w1': p['moe_w1'][l].astype(BF16),
        'moe_w3': p['moe_w3'][l].astype(BF16),
        'moe_w2': p['moe_w2'][l].astype(BF16),
    }
    w_r = jnp.concatenate([p['moe_we'][l], p['moe_wg'][l]], axis=1)
    lw['moe_wr'] = jnp.pad(w_r, ((0, 0), (0, LANE - w_r.shape[1])))
    b_r = jnp.concatenate([p['moe_be'][l], p['moe_bg'][l]])
    lw['moe_br'] = jnp.pad(b_r, (0, LANE - b_r.shape[0])).reshape(1, -1)
    return lw


def _rwkv_kernel(pd_ref, shift0_ref, st0_ref, mu_ref, w0_ref, w2_ref, a0_ref, a2_ref, g2_ref,
                 kkw_ref, ka_ref, rk_ref, lng_ref, lnb_ref, hs_ref,
                 y_ref, stn_ref, prev_sc, st_sc, *, t_valid):
    j = pl.program_id(1)
    W = BRANCH_W
    nb, C = pd_ref.shape[0], pd_ref.shape[1]
    R4 = N_HEADS * C

    @pl.when(j == 0)
    def _():
        prev_sc[...] = shift0_ref[...]
        st_sc[...] = st0_ref[...]

    hs = hs_ref[...]
    row = _row_iota((C, 1))
    live = (row + j * C) < t_valid
    rr = _row_iota((R4, R4))
    cc = _col_iota((R4, R4))
    same_head = (rr // C) == (cc // C)
    strict = same_head & ((cc % C) < (rr % C))
    incl = same_head & ((cc % C) <= (rr % C))
    eye = rr == cc
    tile4 = lambda x: jnp.concatenate([x] * N_HEADS, axis=0)
    stack = lambda x: jnp.concatenate([x[:, h * HEAD_DIM:(h + 1) * HEAD_DIM] for h in range(N_HEADS)], axis=0)

    def chain(bi):
        pd = pd_ref[bi]
        prv = jnp.where(row == 0, prev_sc[bi], pltpu.roll(pd, 1, 0))
        ps = pd + (prv - pd) * mu_ref[...]
        r = ps[:, 0:W]
        k = ps[:, W:2 * W]
        v = ps[:, 2 * W:3 * W]
        tail = ps[:, 3 * W:3 * W + RW_TAIL]
        w_log = -_softplus(-(w0_ref[...] + _dot(jnp.tanh(tail), w2_ref[...]))) - 0.5
        logw = -jnp.exp(w_log)
        a = _sigmoid(a0_ref[...] + _dot(tail, a2_ref[...]))
        g = _dot(_sigmoid(tail), g2_ref[...])
        kk = k * kkw_ref[...]
        kk_ss = _dot2(kk * kk, hs)
        yield
        kk = kk / jnp.maximum(jnp.sqrt(kk_ss), 1e-12)
        k = k * (1.0 + (a - 1.0) * ka_ref[...])
        logw = jnp.where(live, logw, 0.0)
        kk = jnp.where(live, kk, 0.0)
        kl = jnp.where(live, k, 0.0)

        c = _cumsum_rows(logw)
        e_pos = jnp.exp(c)
        e_neg = jnp.exp(-c)
        e_end = e_pos[C - 1:C, :]
        a_bd = jnp.where(same_head, tile4(-kk * jnp.exp(c - logw)), 0.0)
        r_bd = jnp.where(same_head, tile4(r * e_pos), 0.0)
        b_t = kk * a * e_neg
        k_t = kl * e_neg
        ar = jnp.concatenate([a_bd, r_bd], axis=0).astype(BF16)
        bk = jnp.concatenate([tile4(b_t), tile4(k_t)], axis=0).astype(BF16)
        m = _dot_nt(ar, bk)
        st = st_sc[bi]
        ars = _dot(ar, st)
        v_st = stack(v)
        yield
        L = jnp.where(strict, m[0:R4, 0:R4], 0.0)
        X = ars[0:R4] + _dot(jnp.where(strict, m[0:R4, R4:2 * R4], 0.0), v_st)
        yield
        s = 1
        while s < C:
            s *= 2
            if s < C:
                lx = _dot(L, jnp.concatenate([L, X], axis=1))
                L, X = lx[:, 0:R4], X + lx[:, R4:R4 + HEAD_DIM]
            else:
                X = X + _dot(L, X)
            yield
        y_st = (ars[R4:2 * R4] + _dot(jnp.where(incl, m[R4:2 * R4, 0:R4], 0.0), X)
                + _dot(jnp.where(incl, m[R4:2 * R4, R4:2 * R4], 0.0), v_st))
        bg_bd = jnp.where(same_head, tile4(b_t * e_end), 0.0)
        kg_bd = jnp.where(same_head, tile4(k_t * e_end), 0.0)
        e_col = jnp.sum(jnp.where(eye, jnp.broadcast_to(e_end, (R4, R4)), 0.0), axis=1, keepdims=True)
        st_sc[bi] = st * e_col + _dot_tn(bg_bd, X) + _dot_tn(kg_bd, v_st)
        yield

        y = jnp.concatenate([y_st[h * C:(h + 1) * C] for h in range(N_HEADS)], axis=1)
        inv_d = 1.0 / HEAD_DIM
        mean = _dot2(y, hs) * inv_d
        dv = y - mean
        var = _dot2(dv * dv, hs) * inv_d
        yn = dv * lax.rsqrt(var + RW_GN_EPS) * lng_ref[...] + lnb_ref[...]
        bonus = _dot2(r * k * rk_ref[...], hs) * v
        y_ref[bi] = (yn + bonus) * g
        prev_sc[bi] = pd[C - 1:C, :]

    chains = [chain(bi) for bi in range(nb)]
    while chains:
        chains = [ch for ch in chains if next(ch, chains) is not chains]

    @pl.when(j == pl.num_programs(1) - 1)
    def _():
        stn_ref[...] = st_sc[...]


def _rwkv(pd, shift0, s0, lw, t_valid, nb):
    b, tp, _ = pd.shape
    W = BRANCH_W
    C = RW_CHUNK
    st0 = jnp.swapaxes(s0, 2, 3).reshape(b, W, HEAD_DIM)
    full = lambda a: pl.BlockSpec(a.shape, lambda i, j: (0,) * a.ndim)
    ws = (lw['rw_mu'], lw['rw_w0'], lw['rw_w2p'], lw['rw_a0'], lw['rw_a2p'], lw['rw_g2p'],
          lw['rw_kk'], lw['rw_ka'], lw['rw_rk'], lw['rw_ln_g'], lw['rw_ln_b'], lw['head_ones'])
    y, stn = pl.pallas_call(
        functools.partial(_rwkv_kernel, t_valid=t_valid),
        grid=(b // nb, tp // C),
        in_specs=[pl.BlockSpec((nb, C, D_RW_IN), lambda i, j: (i, j, 0)),
                  pl.BlockSpec((nb, 1, D_RW_IN), lambda i, j: (i, 0, 0)),
                  pl.BlockSpec((nb, W, HEAD_DIM), lambda i, j: (i, 0, 0))] + [full(a) for a in ws],
        out_specs=[pl.BlockSpec((nb, C, W), lambda i, j: (i, j, 0)),
                   pl.BlockSpec((nb, W, HEAD_DIM), lambda i, j: (i, 0, 0))],
        out_shape=[jax.ShapeDtypeStruct((b, tp, W), F32),
                   jax.ShapeDtypeStruct((b, W, HEAD_DIM), F32)],
        scratch_shapes=[pltpu.VMEM((nb, 1, D_RW_IN), F32), pltpu.VMEM((nb, W, HEAD_DIM), F32)],
        compiler_params=_cparams(("parallel", "arbitrary")),
        name="rwkv7",
    )(pd, shift0, st0, *ws)
    return y, jnp.swapaxes(stn.reshape(b, N_HEADS, HEAD_DIM, HEAD_DIM), 2, 3)


ATT_BLK = 256
FOX_KEY_BLK = 1024


def _suffix_ones(n):
    i = jnp.arange(n)
    return jnp.concatenate([(i[:, None] > i[None, :]), jnp.ones((n, n), bool)], axis=1).astype(BF16)


def _sb_prompt_kernel(q_ref, k_ref, v_ref, m_ref, o_ref, run_sc, acc_sc):
    i = pl.program_id(1)
    T = ATT_BLK
    msuf = m_ref[...]
    lower = _col_iota((T, T)) < _row_iota((T, T))
    heads = [slice(h * HEAD_DIM, (h + 1) * HEAD_DIM) for h in range(N_HEADS)]
    run_sc[...] = jnp.zeros_like(run_sc)
    acc_sc[...] = jnp.zeros_like(acc_sc)

    def block(kj, diag):
        start = pl.multiple_of(kj * T, T)
        zs = [_dot_nt(q_ref[0, :, sl], k_ref[0, pl.ds(start, T), sl]) for sl in heads]
        log_betas = [jnp.minimum(z, 0.0) - jnp.log(1.0 + jnp.exp(-jnp.abs(z))) for z in zs]
        lss = [lb - z for lb, z in zip(log_betas, zs)]
        if diag:
            lss = [jnp.where(lower, ls, 0.0) for ls in lss]
        sts = [_dot(ls, msuf) for ls in lss]
        ws = [jnp.exp(lb + st + run_sc[h]) for h, (lb, st) in enumerate(zip(log_betas, sts))]
        if diag:
            ws = [jnp.where(lower, w, 0.0) for w in ws]
        for h, sl in enumerate(heads):
            acc_sc[h] += _dot(ws[h], v_ref[0, pl.ds(start, T), sl])
            run_sc[h] += sts[h][:, 0:1] + lss[h][:, 0:1]

    block(i, True)

    def body(jj, carry):
        block(i - 1 - jj, False)
        return carry

    lax.fori_loop(0, i, body, 0)
    o_ref[0] = jnp.concatenate([acc_sc[h] for h in range(N_HEADS)], axis=1)


def _sb_prompt(q, k, v):
    b, t, W = q.shape
    msuf = _suffix_ones(ATT_BLK)[:, :ATT_BLK]
    return pl.pallas_call(
        _sb_prompt_kernel,
        grid=(b, t // ATT_BLK),
        in_specs=[pl.BlockSpec((1, ATT_BLK, W), lambda i, j: (i, j, 0)),
                  pl.BlockSpec((1, t, W), lambda i, j: (i, 0, 0)),
                  pl.BlockSpec((1, t, W), lambda i, j: (i, 0, 0)),
                  pl.BlockSpec(msuf.shape, lambda i, j: (0, 0))],
        out_specs=pl.BlockSpec((1, ATT_BLK, W), lambda i, j: (i, j, 0)),
        out_shape=jax.ShapeDtypeStruct((b, t, W), F32),
        scratch_shapes=[pltpu.VMEM((N_HEADS, ATT_BLK, 1), F32), pltpu.VMEM((N_HEADS, ATT_BLK, HEAD_DIM), F32)],
        compiler_params=_cparams(("parallel", "arbitrary")),
        name="sb_prompt",
    )(q, k, v, msuf)


def _fox_prompt_kernel(q_ref, k_ref, v_ref, fq_ref, fk_ref, o_ref, m_sc, acc_sc):
    i = pl.program_id(1)
    T = ATT_BLK
    TK = FOX_KEY_BLK
    last = (i * T) // TK
    heads = [slice(h * HEAD_DIM, (h + 1) * HEAD_DIM) for h in range(N_HEADS)]
    m_sc[...] = jnp.full_like(m_sc, -1e30)
    acc_sc[...] = jnp.zeros_like(acc_sc)

    def block(kj, diag):
        start = pl.multiple_of(kj * TK, TK)
        keys = pl.ds(start, TK)
        ss = [_dot_nt(q_ref[0, :, sl], k_ref[0, keys, sl]) + (fq_ref[0, :, h:h + 1] - fk_ref[0, h:h + 1, keys])
              for h, sl in enumerate(heads)]
        if diag:
            visible = (_col_iota((T, TK)) + start) <= (_row_iota((T, TK)) + i * T)
            ss = [jnp.where(visible, s, -jnp.inf) for s in ss]
        m_olds = [m_sc[h] for h in range(N_HEADS)]
        m_news = [jnp.maximum(m, jnp.max(s, axis=1, keepdims=True)) for m, s in zip(m_olds, ss)]
        ps = [jnp.exp(s - m) for s, m in zip(ss, m_news)]
        for h in range(N_HEADS):
            alpha = jnp.exp(m_olds[h] - m_news[h])
            acc_sc[h] = acc_sc[h] * alpha + _dot(ps[h], v_ref[0, keys, h * LANE:(h + 1) * LANE])
            m_sc[h] = m_news[h]

    block(last, True)

    def body(jj, carry):
        block(last - 1 - jj, False)
        return carry

    lax.fori_loop(0, last, body, 0)
    o_ref[0] = jnp.concatenate([acc_sc[h][:, 0:HEAD_DIM] / acc_sc[h][:, HEAD_DIM:LANE] for h in range(N_HEADS)],
                               axis=1)


def _fox_prompt(q, k, v_ones, f_col, f_row):
    b, t, W = q.shape
    return pl.pallas_call(
        _fox_prompt_kernel,
        grid=(b, t // ATT_BLK),
        in_specs=[pl.BlockSpec((1, ATT_BLK, W), lambda i, j: (i, j, 0)),
                  pl.BlockSpec((1, t, W), lambda i, j: (i, 0, 0)),
                  pl.BlockSpec((1, t, N_HEADS * LANE), lambda i, j: (i, 0, 0)),
                  pl.BlockSpec((1, ATT_BLK, 8), lambda i, j: (i, j, 0)),
                  pl.BlockSpec((1, 8, t), lambda i, j: (i, 0, 0))],
        out_specs=pl.BlockSpec((1, ATT_BLK, W), lambda i, j: (i, j, 0)),
        out_shape=jax.ShapeDtypeStruct((b, t, W), F32),
        scratch_shapes=[pltpu.VMEM((N_HEADS, ATT_BLK, 1), F32), pltpu.VMEM((N_HEADS, ATT_BLK, LANE), F32)],
        compiler_params=_cparams(("parallel", "arbitrary")),
        name="fox_prompt",
    )(q, k, v_ones, f_col, f_row)


def _cumsum_kernel(x_ref, o_ref):
    n = x_ref.shape[2] // LANE
    parts = [_cumsum_lanes(x_ref[0, :, c * LANE:(c + 1) * LANE]) for c in range(n)]
    carry = jnp.zeros((8, LANE), F32)
    for c, s in enumerate(parts):
        o_ref[0, :, c * LANE:(c + 1) * LANE] = s + carry
        carry = carry + jnp.broadcast_to(s[:, LANE - 1:LANE], s.shape)


def _cumsum_prompt(logf_row):
    b, _, t = logf_row.shape
    return pl.pallas_call(
        _cumsum_kernel,
        grid=(b,),
        in_specs=[pl.BlockSpec((1, 8, t), lambda i: (i, 0, 0))],
        out_specs=pl.BlockSpec((1, 8, t), lambda i: (i, 0, 0)),
        out_shape=jax.ShapeDtypeStruct((b, 8, t), F32),
        compiler_params=_cparams(("parallel",)),
        name="logf_cumsum",
    )(logf_row)


PAGES_PER_STEP = 8
PAGE_SLOTS = 3
QROWS = N_HEADS * 8


def _f_sample_kernel(pt_ref, pool_ref, new_ref, o_ref, x_sc):
    i = pl.program_id(0)
    n_pages = pt_ref.shape[1]
    n = n_pages + 1

    def gather(j, carry):
        p = pt_ref[i, j]
        tile = pool_ref[0, p // 2]
        x_sc[j] = jnp.where(p % 2 == 1, pltpu.roll(tile, N_HEADS, 0), tile)
        return carry

    lax.fori_loop(0, n_pages, gather, 0, unroll=8)
    x_sc[n_pages] = new_ref[0]
    c = _cumsum_lanes(x_sc[...])
    tot = jnp.broadcast_to(c[:, :, LANE - 1:LANE], c.shape)
    inc = tot
    s = 1
    while s < n:
        inc = inc + jnp.concatenate([jnp.zeros((s, 8, LANE), F32), inc[0:n - s]], axis=0)
        s *= 2
    o_ref[0] = c + (inc - tot)


def _f_sample(pool_pairs, logf_new, page_table, layer):
    b, n_pages = page_table.shape
    return pl.pallas_call(
        _f_sample_kernel,
        grid_spec=pltpu.PrefetchScalarGridSpec(
            num_scalar_prefetch=1,
            grid=(b,),
            in_specs=[pl.BlockSpec((1,) + pool_pairs.shape[1:], lambda i, pt: (layer, 0, 0, 0)),
                      pl.BlockSpec((1, 8, LANE), lambda i, pt: (i, 0, 0))],
            out_specs=pl.BlockSpec((1, n_pages + 1, 8, LANE), lambda i, pt: (i, 0, 0, 0)),
            scratch_shapes=[pltpu.VMEM((n_pages + 1, 8, LANE), F32)]),
        out_shape=jax.ShapeDtypeStruct((b, n_pages + 1, 8, LANE), F32),
        compiler_params=_cparams(("arbitrary",)),
        name="logf_cumsum_paged",
    )(page_table, pool_pairs, logf_new)


def _pages_t(cache):
    l, n, p, h, d = cache.shape
    return jnp.transpose(cache, (0, 1, 3, 4, 2)).reshape(l, n, h * d, p)


def _logf_pairs(cache_logf):
    l, n, p, h = cache_logf.shape
    return jnp.swapaxes(cache_logf, 2, 3).reshape(l, n // 2, 2 * h, p)


def _rows_t(x):
    return jnp.pad(jnp.swapaxes(x, 1, 2), ((0, 0), (0, 0), (0, LANE - x.shape[1])))


def _head_mask():
    return (jnp.arange(QROWS)[:, None] // 8) == (jnp.arange(BRANCH_W)[None, :] // HEAD_DIM)


def _heads_q(q):
    b, t, _ = q.shape
    q8 = jnp.pad(q, ((0, 0), (0, 8 - t), (0, 0)))
    return jnp.where(_head_mask(), jnp.tile(q8, (1, N_HEADS, 1)), 0).astype(BF16)


def _heads_out(o, t):
    b = o.shape[0]
    o4 = jnp.where(_head_mask(), o, 0.0).reshape(b, N_HEADS, 8, BRANCH_W)
    return jnp.sum(o4, axis=1)[:, :t, :]


def _sample_attn_kernel(pt_ref, qs_ref, qf_ref, fq_ref, ksn_ref, vsn_ref, kfn_ref, vfn_ref, f_ref, m_ref,
                        ck_sb, cv_sb, ck_fx, cv_fx, os_ref, of_ref,
                        kb_sb, vb_sb, kb_fx, vb_fx, sems, run_sc, accs_sc, m_sc, l_sc, accf_sc, *, layer, n_new):
    G = PAGES_PER_STEP
    b = pl.program_id(0)
    n_pages = pt_ref.shape[1]
    n_groups = n_pages // G
    caches = (ck_sb, cv_sb, ck_fx, cv_fx)
    bufs = (kb_sb, vb_sb, kb_fx, vb_fx)
    qs = qs_ref[0]
    qf = qf_ref[0]
    fq = fq_ref[0]
    msuf = m_ref[...]

    def page_of(grp, i):
        return n_pages - 1 - (grp * G + i)

    def copies(grp, slot):
        return [pltpu.make_async_copy(caches[a].at[layer, pt_ref[b, page_of(grp, i)]], bufs[a].at[slot, i],
                                      sems.at[slot, a])
                for a in range(len(caches)) for i in range(G)]

    def sb_blocks(kts, vts, mask):
        zs = [jnp.dot(qs, kt.astype(BF16), preferred_element_type=F32) for kt in kts]
        ls = [-_softplus(z) for z in zs]
        if mask is not None:
            ls = [jnp.where(mask, l, 0.0) for l in ls]
        st = _dot2(jnp.concatenate(ls, axis=0) if len(ls) > 1 else ls[0], msuf)
        run = run_sc[...]
        acc = accs_sc[...]
        for g in range(len(kts)):
            sg = st[g * QROWS:(g + 1) * QROWS]
            w = jnp.exp(zs[g] + ls[g] + sg[:, 0:LANE] + run)
            if mask is not None:
                w = jnp.where(mask, w, 0.0)
            acc = acc + _dot_nt(w, vts[g])
            run = run + sg[:, LANE:2 * LANE]
        run_sc[...] = run
        accs_sc[...] = acc

    def head_rows(f):
        return jnp.concatenate([jnp.broadcast_to(f[h:h + 1, :], (8, LANE)) for h in range(N_HEADS)], axis=0)

    def fox_blocks(kts, vts, fks, mask):
        ss = [jnp.dot(qf, kt.astype(BF16), preferred_element_type=F32) + (fq - head_rows(fk))
              for kt, fk in zip(kts, fks)]
        if mask is not None:
            ss = [jnp.where(mask, s, -jnp.inf) for s in ss]
        s_max = ss[0]
        for s in ss[1:]:
            s_max = jnp.maximum(s_max, s)
        m_old = m_sc[...]
        m_new = jnp.maximum(m_old, jnp.max(s_max, axis=1, keepdims=True))
        alpha = jnp.exp(m_old - m_new)
        ps = [jnp.exp(s - m_new) for s in ss]
        p_sum = ps[0]
        pv = _dot_nt(ps[0], vts[0])
        for p, vt in zip(ps[1:], vts[1:]):
            p_sum = p_sum + p
            pv = pv + _dot_nt(p, vt)
        l_sc[...] = l_sc[...] * alpha + jnp.sum(p_sum, axis=1, keepdims=True)
        accf_sc[...] = accf_sc[...] * alpha + pv
        m_sc[...] = m_new

    for grp in range(PAGE_SLOTS - 1):
        for c in copies(grp, grp):
            c.start()

    run_sc[...] = jnp.zeros_like(run_sc)
    accs_sc[...] = jnp.zeros_like(accs_sc)
    m_sc[...] = jnp.full_like(m_sc, -1e30)
    l_sc[...] = jnp.zeros_like(l_sc)
    accf_sc[...] = jnp.zeros_like(accf_sc)
    t = _row_iota((QROWS, LANE)) % 8
    s = _col_iota((QROWS, LANE))
    sb_blocks([ksn_ref[0]], [vsn_ref[0]], (s < t) & (s < n_new))
    fox_blocks([kfn_ref[0]], [vfn_ref[0]], [f_ref[0, n_pages]], (s <= t) & (s < n_new))

    def group(grp, carry):
        slot = grp % PAGE_SLOTS
        ahead = grp + (PAGE_SLOTS - 1)

        @pl.when(ahead < n_groups)
        def _():
            for c in copies(ahead, ahead % PAGE_SLOTS):
                c.start()

        for c in copies(grp, slot):
            c.wait()
        sb_blocks([kb_sb[slot, i] for i in range(G)], [vb_sb[slot, i] for i in range(G)], None)
        fox_blocks([kb_fx[slot, i] for i in range(G)], [vb_fx[slot, i] for i in range(G)],
                   [f_ref[0, page_of(grp, i)] for i in range(G)], None)
        return carry

    lax.fori_loop(0, n_groups, group, 0)
    os_ref[0] = accs_sc[...]
    of_ref[0] = accf_sc[...] / l_sc[...]


def _sample_attn(q_sb, k_sb, v_sb, q_fx, k_fx, v_fx, logf_new, caches, pool_pairs, page_table, layer):
    b, t, _ = q_sb.shape
    n_pages = page_table.shape[1]
    G = PAGES_PER_STEP
    lf = jnp.pad(jnp.swapaxes(logf_new, 1, 2), ((0, 0), (0, 8 - N_HEADS), (0, LANE - t)))
    f_all = _f_sample(pool_pairs, lf, page_table, layer)
    f_q = f_all[:, n_pages, :N_HEADS, :8]
    f_q = jnp.broadcast_to(f_q.reshape(b, QROWS, 1), (b, QROWS, LANE))
    msuf = _suffix_ones(LANE)
    q_spec = pl.BlockSpec((1, QROWS, BRANCH_W), lambda i, pt: (i, 0, 0))
    new_spec = pl.BlockSpec((1, BRANCH_W, LANE), lambda i, pt: (i, 0, 0))
    any_spec = pl.BlockSpec(memory_space=pl.ANY)
    page_buf = pltpu.VMEM((PAGE_SLOTS, G, BRANCH_W, PAGE), F32)
    o_sb, o_fx = pl.pallas_call(
        functools.partial(_sample_attn_kernel, layer=layer, n_new=t),
        grid_spec=pltpu.PrefetchScalarGridSpec(
            num_scalar_prefetch=1,
            grid=(b,),
            in_specs=[q_spec, q_spec, pl.BlockSpec((1, QROWS, LANE), lambda i, pt: (i, 0, 0)),
                      new_spec, new_spec, new_spec, new_spec,
                      pl.BlockSpec((1, n_pages + 1, 8, LANE), lambda i, pt: (i, 0, 0, 0)),
                      pl.BlockSpec(msuf.shape, lambda i, pt: (0, 0)),
                      any_spec, any_spec, any_spec, any_spec],
            out_specs=[q_spec, q_spec],
            scratch_shapes=[page_buf, page_buf, page_buf, page_buf, pltpu.SemaphoreType.DMA((PAGE_SLOTS, 4)),
                            pltpu.VMEM((QROWS, LANE), F32), pltpu.VMEM((QROWS, BRANCH_W), F32),
                            pltpu.VMEM((QROWS, 1), F32), pltpu.VMEM((QROWS, 1), F32),
                            pltpu.VMEM((QROWS, BRANCH_W), F32)]),
        out_shape=[jax.ShapeDtypeStruct((b, QROWS, BRANCH_W), F32)] * 2,
        compiler_params=_cparams(("arbitrary",)),
        name="sample_attn",
    )(page_table, _heads_q(q_sb), _heads_q(q_fx), f_q, _rows_t(k_sb), _rows_t(v_sb), _rows_t(k_fx), _rows_t(v_fx),
      f_all, msuf, *caches)
    return _heads_out(o_sb, t), _heads_out(o_fx, t)


def _merge_kernel(x_ref, ya_ref, yb_ref, yc_ref, yd_ref, g_ref, wg_ref, wb_ref, wo_ref, o_ref):
    x = x_ref[...]
    xn = _rmsnorm(x, g_ref[...]).astype(BF16)
    m = None
    for n, y_ref in enumerate((ya_ref, yb_ref, yc_ref, yd_ref)):
        gate = _sigmoid(jnp.dot(xn, wg_ref[:, n * D_MODEL:(n + 1) * D_MODEL], preferred_element_type=F32))
        term = gate * _dot(y_ref[...], wb_ref[n])
        m = term if m is None else m + term
    o_ref[...] = x + _dot(m, wo_ref[...])


def _merge(x2d, ys, lw, tm):
    n = x2d.shape[0]
    row = lambda w: pl.BlockSpec((tm, w), lambda i: (i, 0))
    full = lambda a: pl.BlockSpec(a.shape, lambda i: (0,) * a.ndim)
    ws = (lw['norm1_g'], lw['w_gate'], lw['w_branch'], lw['w_out'])
    layer = lw['layer']
    wg_spec = pl.BlockSpec((None, D_MODEL, N_GATE), lambda i: (layer, 0, 0))
    return pl.pallas_call(
        _merge_kernel,
        grid=(n // tm,),
        in_specs=[row(D_MODEL)] + [row(BRANCH_W)] * 4 + [full(ws[0]), wg_spec, full(ws[2]), full(ws[3])],
        out_specs=row(D_MODEL),
        out_shape=jax.ShapeDtypeStruct((n, D_MODEL), F32),
        compiler_params=_cparams(("parallel",)),
        name="merge",
    )(x2d, *ys, *ws)


MOE_EXPERTS_PER_STEP = 4


def _router_gate(logits):
    lane = _col_iota(logits.shape)
    lane_f = lane.astype(F32)
    big = float(LANE)
    neg = -jnp.inf
    gl = jnp.where((lane >= N_EXPERTS) & (lane < N_EXPERTS + N_GROUPS), logits, neg)
    g_max = jnp.max(gl, axis=1, keepdims=True)
    g_idx = jnp.min(jnp.where(gl == g_max, lane_f, big), axis=1, keepdims=True) - float(N_EXPERTS)
    g_w = 1.0 / jnp.sum(jnp.exp(gl - g_max), axis=1, keepdims=True)
    per_group = N_EXPERTS // N_GROUPS
    lo = g_idx * float(per_group)
    el = jnp.where((lane_f >= lo) & (lane_f < lo + float(per_group)), logits, neg)
    m1 = jnp.max(el, axis=1, keepdims=True)
    i1 = jnp.min(jnp.where(el == m1, lane_f, big), axis=1, keepdims=True)
    el2 = jnp.where(lane_f == i1, neg, el)
    m2 = jnp.max(el2, axis=1, keepdims=True)
    i2 = jnp.min(jnp.where(el2 == m2, lane_f, big), axis=1, keepdims=True)
    d = jnp.exp(m2 - m1)
    w1 = g_w / (1.0 + d)
    return jnp.where(lane_f == i1, w1, 0.0) + jnp.where(lane_f == i2, w1 * d, 0.0)


def _moe_kernel(x_ref, g_ref, wr_ref, br_ref, w1_ref, w3_ref, w2_ref, o_ref, xn_sc, gate_sc, acc_sc):
    e = pl.program_id(1)

    @pl.when(e == 0)
    def _():
        x = x_ref[...]
        xn = _rmsnorm(x, g_ref[...])
        xn_sc[...] = xn.astype(BF16)
        logits = jnp.dot(xn, wr_ref[...], preferred_element_type=F32, precision=lax.Precision.HIGHEST) + br_ref[...]
        gate_sc[...] = _router_gate(logits)
        acc_sc[...] = x

    xn = xn_sc[...]
    n_e = w1_ref.shape[0]
    gate = gate_sc[...]
    lane = _col_iota(gate.shape)
    a_s = [jnp.dot(xn, w1_ref[k], preferred_element_type=F32) for k in range(n_e)]
    b_s = [jnp.dot(xn, w3_ref[k], preferred_element_type=F32) for k in range(n_e)]
    g_s = [jnp.sum(jnp.where(lane == e * n_e + k, gate, 0.0), axis=1, keepdims=True) for k in range(n_e)]
    h = jnp.concatenate([(a * _sigmoid(a) * b * g).astype(BF16) for a, b, g in zip(a_s, b_s, g_s)], axis=1)
    acc_sc[...] += jnp.dot(h, w2_ref[...].reshape(n_e * D_EXP, D_MODEL), preferred_element_type=F32)

    @pl.when(e == pl.num_programs(1) - 1)
    def _():
        o_ref[...] = acc_sc[...]


def _moe(x2d, lw, tm):
    n = x2d.shape[0]
    row = pl.BlockSpec((tm, D_MODEL), lambda i, e: (i, 0))
    full = lambda a: pl.BlockSpec(a.shape, lambda i, e: (0,) * a.ndim)
    return pl.pallas_call(
        _moe_kernel,
        grid=(n // tm, N_EXPERTS // MOE_EXPERTS_PER_STEP),
        in_specs=[row, full(lw['norm2_g']), full(lw['moe_wr']), full(lw['moe_br']),
                  pl.BlockSpec((MOE_EXPERTS_PER_STEP, D_MODEL, D_EXP), lambda i, e: (e, 0, 0)),
                  pl.BlockSpec((MOE_EXPERTS_PER_STEP, D_MODEL, D_EXP), lambda i, e: (e, 0, 0)),
                  pl.BlockSpec((MOE_EXPERTS_PER_STEP, D_EXP, D_MODEL), lambda i, e: (e, 0, 0))],
        out_specs=row,
        out_shape=jax.ShapeDtypeStruct((n, D_MODEL), F32),
        scratch_shapes=[pltpu.VMEM((tm, D_MODEL), BF16), pltpu.VMEM((tm, LANE), F32), pltpu.VMEM((tm, D_MODEL), F32)],
        compiler_params=_cparams(("parallel", "arbitrary")),
        name="moe",
    )(x2d, lw['norm2_g'], lw['moe_wr'], lw['moe_br'], lw['moe_w1'], lw['moe_w3'], lw['moe_w2'])


def _pad_rows(a, mult):
    t = a.shape[1]
    tp = -(-t // mult) * mult
    return a if tp == t else jnp.pad(a, ((0, 0), (0, tp - t), (0, 0)))


def _layer(x, lw, state, paged):
    b, t, _ = x.shape
    n = b * t
    W = BRANCH_W
    conv0, h0, shift0, wkv0 = state
    tm = min(n, 256)
    x2d = x.reshape(n, D_MODEL)
    state_t = paged is None and t % tm == 0
    (lru_xy, rw_in, sbq, sbk, sbv, sbkb, sbvb, fq, fk, fv, fkb, fvb, logf_pad) = _proj(
        x2d, lw, tm, t if state_t else None)
    r3 = lambda a: a.reshape(b, t, a.shape[-1])
    logf = r3(logf_pad[:, :N_HEADS])

    tt = 256 if t % 256 == 0 else 8
    y_a, conv_n, h_n = _lru(_pad_rows(r3(lru_xy), tt), jnp.pad(conv0, ((0, 0), (8 - (CONV_W - 1), 0), (0, 0))),
                            h0[:, None, :], lw, tt, t)
    rw3 = r3(rw_in)
    y_d, wkv_n = _rwkv(_pad_rows(rw3, RW_CHUNK), shift0[:, None, :], wkv0, lw, t, 4)

    if paged is None:
        y_b = _sb_prompt(r3(sbq), r3(sbkb), r3(sbvb))
        f_row = _cumsum_prompt(jnp.pad(jnp.swapaxes(logf, 1, 2), ((0, 0), (0, 8 - N_HEADS), (0, 0))))
        y_c = _fox_prompt(r3(fq), r3(fkb), r3(fvb), jnp.swapaxes(f_row, 1, 2), f_row)
    else:
        sb_kt, sb_vt, fox_kt, fox_vt, logf_pairs, page_table, layer = paged
        y_b, y_c = _sample_attn(r3(sbq), r3(sbk), r3(sbv), r3(fq), r3(fk), r3(fv), logf,
                                (sb_kt, sb_vt, fox_kt, fox_vt), logf_pairs, page_table, layer)

    ys = (y_a[:, :t].reshape(n, W), y_b.reshape(n, W), y_c.reshape(n, W), y_d[:, :t].reshape(n, W))
    x1 = _merge(x2d, ys, lw, tm)
    x2 = _moe(x1, lw, min(n, 1024))
    if state_t:
        hd = lambda a: jnp.transpose(a.reshape(b, N_HEADS, HEAD_DIM, t), (0, 3, 1, 2))
    else:
        hd = lambda a: a.reshape(b, t, N_HEADS, HEAD_DIM)
    new_state = (hd(sbk), hd(sbv), hd(fk), hd(fv), logf, conv_n[:, 8 - (CONV_W - 1):], h_n[:, 0], rw3[:, t - 1], wkv_n)
    return x2.reshape(b, t, D_MODEL), new_state


def kernel(x_prompt, x_sample, cache_sb_k, cache_sb_v, cache_fox_k, cache_fox_v, cache_fox_logf, state_lru_conv, state_lru_h, state_rwkv_shift, state_rwkv_wkv, page_table, norm1_g, w_in, lru_conv_w, lru_conv_b, lru_wa, lru_ba, lru_wi, lru_bi, lru_lambda, fox_qnorm_g, fox_knorm_g, fox_fbias, rw_mu, rw_w0, rw_w2, rw_a0, rw_a2, rw_g2, rw_kk, rw_ka, rw_rk, rw_ln_g, rw_ln_b, w_branch, w_out, norm2_g, moe_wg, moe_bg, moe_we, moe_be, moe_w1, moe_w3, moe_w2):
    params = dict(norm1_g=norm1_g, w_in=w_in, lru_conv_w=lru_conv_w, lru_conv_b=lru_conv_b, lru_wa=lru_wa,
                  lru_ba=lru_ba, lru_wi=lru_wi, lru_bi=lru_bi, lru_lambda=lru_lambda, fox_qnorm_g=fox_qnorm_g,
                  fox_knorm_g=fox_knorm_g, fox_fbias=fox_fbias, rw_mu=rw_mu, rw_w0=rw_w0, rw_w2=rw_w2, rw_a0=rw_a0,
                  rw_a2=rw_a2, rw_g2=rw_g2, rw_kk=rw_kk, rw_ka=rw_ka, rw_rk=rw_rk, rw_ln_g=rw_ln_g, rw_ln_b=rw_ln_b,
                  w_branch=w_branch, w_out=w_out, norm2_g=norm2_g, moe_wg=moe_wg, moe_bg=moe_bg, moe_we=moe_we,
                  moe_be=moe_be, moe_w1=moe_w1, moe_w3=moe_w3, moe_w2=moe_w2)
    depth = w_in.shape[0]
    bp = x_prompt.shape[0]
    sb_kt, sb_vt = _pages_t(cache_sb_k), _pages_t(cache_sb_v)
    fox_kt, fox_vt = _pages_t(cache_fox_k), _pages_t(cache_fox_v)
    logf_pairs = _logf_pairs(cache_fox_logf)
    prompt_init = (jnp.zeros((bp, CONV_W - 1, BRANCH_W), F32), jnp.zeros((bp, BRANCH_W), F32),
                   jnp.zeros((bp, D_RW_IN), F32), jnp.zeros((bp, N_HEADS, HEAD_DIM, HEAD_DIM), F32))
    w_proj, w_gate = _regroup_w_in(w_in)
    y_p, y_s = x_prompt, x_sample
    st_p, st_s = [], []
    for l in range(depth):
        lw = _prep_layer(params, l, w_proj, w_gate)
        y_p, sp = _layer(y_p, lw, prompt_init, None)
        y_s, ss = _layer(y_s, lw, (state_lru_conv[l], state_lru_h[l], state_rwkv_shift[l], state_rwkv_wkv[l]),
                         (sb_kt, sb_vt, fox_kt, fox_vt, logf_pairs, page_table, l))
        st_p.append(sp)
        st_s.append(ss)
    stack = lambda sts: [jnp.stack([s[i] for s in sts], axis=0) for i in range(9)]
    return (y_p, y_s, *stack(st_p), *stack(st_s))
```
